```python
import math
import jax, jax.numpy as jnp
from jax import lax
import numpy as np

D_MODEL = 2048
BATCH = 16
SEQ = 2048
DEPTH = 2

N_MIXERS = 2
POOL_WINDOWS = (2, 4, 8, 16)
N_POOL_GROUPS = 4
POOL_GROUP_DIM = D_MODEL // N_POOL_GROUPS
ATTN_PATTERNS = ((128, 1), (512, 4), (2048, 16))
N_ATTN_GROUPS = 3
HEAD_DIM = 128
N_SLOTS = 8
GROUP_WIDTH = N_SLOTS * HEAD_DIM
QKV_WIDTH = 3 * N_ATTN_GROUPS * GROUP_WIDTH
BLOCK = 128
N_TOTAL_HEADS = N_ATTN_GROUPS * N_SLOTS
D_FF = 4 * D_MODEL
LN_EPS = 1e-5
DEEPNORM_ALPHA = (2 * DEPTH) ** 0.25
DEEPNORM_BETA = (8 * DEPTH) ** -0.25
N_POOL_LAYERS = (DEPTH + 1) // 2
N_ATTN_LAYERS = DEPTH // 2

kernel_name = "hybrid_pool_dilated_attn_deepnorm"


def layer_norm(x, g, b):
    xf = x.astype(jnp.float32)
    mu = jnp.mean(xf, axis=-1, keepdims=True)
    xc = xf - mu
    var = jnp.mean(xc * xc, axis=-1, keepdims=True)
    y = xc * lax.rsqrt(var + LN_EPS) * g.astype(jnp.float32) + b.astype(jnp.float32)
    return y.astype(x.dtype)


def alibi_slopes():
    n = N_TOTAL_HEADS
    return jnp.exp2(-8.0 * jnp.arange(1, n + 1, dtype=jnp.float32) / n).reshape(N_ATTN_GROUPS, N_SLOTS)


def pool_mixer(x, w_in, w_group, scale, w_out):
    B, S, D = x.shape
    u = (x @ w_in).reshape(B, S, N_POOL_GROUPS, POOL_GROUP_DIM)
    uf = u.astype(jnp.float32)
    cs = jnp.cumsum(uf, axis=1)
    t = jnp.arange(S)
    outs = []
    for g, w in enumerate(POOL_WINDOWS):
        c = cs[:, :, g]
        lagged = jnp.pad(c, ((0, 0), (w, 0), (0, 0)))[:, :S]
        cnt = jnp.minimum(t + 1, w).astype(jnp.float32)[:, None]
        outs.append((c - lagged) / cnt - uf[:, :, g])
    p = jnp.stack(outs, axis=2).astype(x.dtype)
    y = jnp.einsum('bsgc,gcd->bsgd', p, w_group).reshape(B, S, D) * scale
    return y @ w_out


def strided_window_attention(q, k, v, dil, n_back, slopes):
    B, S, H, E = q.shape
    L = S // dil
    nb = -(-L // BLOCK)
    Lp = nb * BLOCK
    N = B * dil

    def to_blocks(a):
        a = a.reshape(B, L, dil, H, E).transpose(0, 2, 1, 3, 4).reshape(N, L, H, E)
        a = jnp.pad(a, ((0, 0), (0, Lp - L), (0, 0), (0, 0)))
        return a.reshape(N, nb, BLOCK, H, E)

    def with_prev(a):
        prev = jnp.pad(a, ((0, 0), (1, 0), (0, 0), (0, 0), (0, 0)))[:, :nb]
        return jnp.concatenate([prev, a], axis=2)

    def from_blocks(a):
        rest = a.shape[3:]
        a = a.reshape((B, dil, Lp) + rest)[:, :, :L]
        a = jnp.moveaxis(a, 1, 2)
        return a.reshape((B, S) + rest)

    qb = to_blocks(q)
    kw = with_prev(to_blocks(k))
    vw = with_prev(to_blocks(v))
    s = jnp.einsum('nbqhe,nbkhe->nbhqk', qb, kw, preferred_element_type=jnp.float32)
    s = s * (1.0 / math.sqrt(E))
    qi = jnp.arange(BLOCK)[:, None]
    ki = jnp.arange(2 * BLOCK)[None, :]
    dist = qi + BLOCK - ki
    blk = jnp.arange(nb)[:, None, None]
    valid = ((dist >= 0) & (dist <= n_back))[None] & ((blk > 0) | (ki[None] >= BLOCK))
    bias = -slopes[:, None, None] * (dist * dil).astype(jnp.float32)[None]
    s = s + bias[None, None]
    s = jnp.where(valid[None, :, None], s, -jnp.inf)
    m = jnp.max(s, axis=-1, keepdims=True)
    e = jnp.exp(s - m)
    den = jnp.sum(e, axis=-1, keepdims=True)
    p = e / den
    lse = (m + jnp.log(den))[..., 0]
    o = jnp.einsum('nbhqk,nbkhe->nbqhe', p.astype(v.dtype), vw,
                   preferred_element_type=jnp.float32)
    return from_blocks(o), from_blocks(jnp.swapaxes(lse, 2, 3))


def dilated_attention(x, w_qkv, w_out):
    B, S, D = x.shape
    qkv = (x @ w_qkv).reshape(B, S, 3, N_ATTN_GROUPS, N_SLOTS, HEAD_DIM)
    slopes = alibi_slopes()
    outs, lses = [], []
    for g, (w, d) in enumerate(ATTN_PATTERNS):
        o, lse = strided_window_attention(qkv[:, :, 0, g], qkv[:, :, 1, g], qkv[:, :, 2, g],
                                          d, w // d, slopes[g])
        outs.append(o)
        lses.append(lse)
    wts = jax.nn.softmax(jnp.stack(lses, axis=0), axis=0)
    merged = jnp.einsum('gbsh,gbshe->bshe', wts, jnp.stack(outs, axis=0)).astype(x.dtype)
    return merged.reshape(B, S, GROUP_WIDTH) @ w_out


def sqrelu_mlp(x, w_up, w_down):
    h = jax.nn.relu(x @ w_up)
    return (h * h) @ w_down


def setup_inputs(seed: int = 0) -> dict:
    key = jax.random.key(seed)
    ks = jax.random.split(key, 14)
    f32 = jnp.float32
    nrm = lambda k, shape, s: jax.random.normal(k, shape, f32) * s
    x = jax.random.normal(ks[0], (BATCH, SEQ, D_MODEL), f32)
    pool_w_in = nrm(ks[1], (N_POOL_LAYERS, D_MODEL, D_MODEL), D_MODEL ** -0.5)
    pool_w_group = nrm(ks[2], (N_POOL_LAYERS, N_POOL_GROUPS, POOL_GROUP_DIM, POOL_GROUP_DIM), POOL_GROUP_DIM ** -0.5)
    pool_scale = 1.0 + nrm(ks[3], (N_POOL_LAYERS, D_MODEL), 0.1)
    pool_w_out = nrm(ks[4], (N_POOL_LAYERS, D_MODEL, D_MODEL), DEEPNORM_BETA * D_MODEL ** -0.5)
    attn_w_qkv = nrm(ks[5], (N_ATTN_LAYERS, D_MODEL, QKV_WIDTH), D_MODEL ** -0.5)
    attn_w_out = nrm(ks[6], (N_ATTN_LAYERS, GROUP_WIDTH, D_MODEL), DEEPNORM_BETA * GROUP_WIDTH ** -0.5)
    mlp_w_up = nrm(ks[7], (DEPTH, D_MODEL, D_FF), D_MODEL ** -0.5)
    mlp_w_down = nrm(ks[8], (DEPTH, D_FF, D_MODEL), DEEPNORM_BETA * D_FF ** -0.5)
    ln_mix_g = 1.0 + nrm(ks[9], (DEPTH, D_MODEL), 0.05)
    ln_mix_b = nrm(ks[10], (DEPTH, D_MODEL), 0.02)
    ln_mlp_g = 1.0 + nrm(ks[11], (DEPTH, D_MODEL), 0.05)
    ln_mlp_b = nrm(ks[12], (DEPTH, D_MODEL), 0.02)
    return {"x": x, "pool_w_in": pool_w_in, "pool_w_group": pool_w_group, "pool_scale": pool_scale,
            "pool_w_out": pool_w_out, "attn_w_qkv": attn_w_qkv, "attn_w_out": attn_w_out,
            "mlp_w_up": mlp_w_up, "mlp_w_down": mlp_w_down, "ln_mix_g": ln_mix_g, "ln_mix_b": ln_mix_b,
            "ln_mlp_g": ln_mlp_g, "ln_mlp_b": ln_mlp_b}


def reference(x, pool_w_in, pool_w_group, pool_scale, pool_w_out, attn_w_qkv, attn_w_out,
              mlp_w_up, mlp_w_down, ln_mix_g, ln_mix_b, ln_mlp_g, ln_mlp_b):
    for i in range(DEPTH):
        j = i // N_MIXERS
        if i % N_MIXERS == 0:
            h = pool_mixer(x, pool_w_in[j], pool_w_group[j], pool_scale[j], pool_w_out[j])
        else:
            h = dilated_attention(x, attn_w_qkv[j], attn_w_out[j])
        x = layer_norm(DEEPNORM_ALPHA * x + h, ln_mix_g[i], ln_mix_b[i])
        x = layer_norm(DEEPNORM_ALPHA * x + sqrelu_mlp(x, mlp_w_up[i], mlp_w_down[i]),
                       ln_mlp_g[i], ln_mlp_b[i])
    return x
```

```python
import functools
import math

import jax
import jax.numpy as jnp
from jax import lax
from jax.experimental import pallas as pl
from jax.experimental.pallas import tpu as pltpu

POOL_WINDOWS = (2, 4, 8, 16)
MAX_POOL_WINDOW = max(POOL_WINDOWS)
ATTN_PATTERNS = ((128, 1), (512, 4), (2048, 16))
HEAD_DIM = 128
ATTN_BLOCK = 128
LN_EPS = 1e-5
MASK_DISTANCE = 1e30

V7X_VMEM_LIMIT_BYTES = 56 * 1024 * 1024
LN_ROW_CHUNK = 32

_F32 = jnp.float32
_BF16 = jnp.bfloat16


def _params(*semantics):
    return pltpu.CompilerParams(dimension_semantics=semantics, vmem_limit_bytes=V7X_VMEM_LIMIT_BYTES)


def _resident(shape):
    return pl.BlockSpec(shape, lambda *_: (0,) * len(shape), pipeline_mode=pl.Buffered(1))


def _residual_layer_norm_inplace(x_ref, h_ref, g_ref, b_ref, alpha):
    gam = g_ref[...]
    bet = b_ref[...]

    def body(c, carry):
        rows = pl.ds(pl.multiple_of(c * LN_ROW_CHUNK, LN_ROW_CHUNK), LN_ROW_CHUNK)
        r = alpha * x_ref[rows, :] + h_ref[rows, :]
        mu = jnp.mean(r, axis=-1, keepdims=True)
        xc = r - mu
        var = jnp.mean(xc * xc, axis=-1, keepdims=True)
        h_ref[rows, :] = xc * lax.rsqrt(var + LN_EPS) * gam + bet
        return carry

    lax.fori_loop(0, x_ref.shape[0] // LN_ROW_CHUNK, body, 0)


def _pool_in_kernel(x_ref, w_ref, p_ref, carry_ref, *, blocks_per_seq):
    tm = x_ref.shape[0]
    gd = w_ref.shape[1] // len(POOL_WINDOWS)
    blk = pl.program_id(0) % blocks_per_seq

    @pl.when(blk == 0)
    def _():
        carry_ref[...] = jnp.zeros_like(carry_ref)

    xb = x_ref[...].astype(_BF16)
    pos = lax.broadcasted_iota(jnp.int32, (tm, 1), 0) + blk * tm
    for g, w in enumerate(POOL_WINDOWS):
        cols = slice(g * gd, (g + 1) * gd)
        u = jnp.dot(xb, w_ref[:, cols], preferred_element_type=_F32)
        t = jnp.concatenate([carry_ref[g], u], axis=0)
        carry_ref[g] = u[tm - MAX_POOL_WINDOW:, :]
        k = 1
        while k < w:
            t = t + pltpu.roll(t, k, 0)
            k *= 2
        inv_cnt = 1.0 / jnp.minimum(pos + 1, w).astype(_F32)
        p_ref[:, cols] = (t[MAX_POOL_WINDOW:, :] * inv_cnt - u).astype(p_ref.dtype)


def _pool_out_kernel(p_ref, x_ref, wg_ref, scale_ref, wo_ref, g_ref, b_ref, o_ref, *, alpha):
    n_groups, gd, _ = wg_ref.shape
    ys = []
    for g in range(n_groups):
        cols = slice(g * gd, (g + 1) * gd)
        y = jnp.dot(p_ref[:, cols], wg_ref[g], preferred_element_type=_F32) * scale_ref[:, cols]
        ys.append(y.astype(_BF16))
    yb = jnp.concatenate(ys, axis=1)
    o_ref[...] = jnp.dot(yb, wo_ref[...], preferred_element_type=_F32)
    _residual_layer_norm_inplace(x_ref, o_ref, g_ref, b_ref, alpha)


def _pool_layer(x, w_in, w_group, scale, w_out, ln_g, ln_b, *, seq, alpha):
    rows, d = x.shape
    n_groups, gd, _ = w_group.shape
    tm = min(512, seq)
    assert seq % tm == 0 and tm >= MAX_POOL_WINDOW and gd * n_groups == d
    row_block = pl.BlockSpec((tm, d), lambda i: (i, 0))
    p = pl.pallas_call(
        functools.partial(_pool_in_kernel, blocks_per_seq=seq // tm),
        out_shape=jax.ShapeDtypeStruct((rows, d), _BF16),
        grid=(rows // tm,),
        in_specs=[row_block, _resident((d, d))],
        out_specs=row_block,
        scratch_shapes=[pltpu.VMEM((n_groups, MAX_POOL_WINDOW, gd), _F32)],
        compiler_params=_params("arbitrary"),
        name="pool_in",
    )(x, w_in.astype(_BF16))
    return pl.pallas_call(
        functools.partial(_pool_out_kernel, alpha=alpha),
        out_shape=jax.ShapeDtypeStruct((rows, d), _F32),
        grid=(rows // tm,),
        in_specs=[row_block, row_block, _resident((n_groups, gd, gd)), _resident((1, d)),
                  _resident((d, d)), _resident((1, d)), _resident((1, d))],
        out_specs=row_block,
        compiler_params=_params("parallel"),
        name="pool_out",
    )(p, x, w_group.astype(_BF16), scale.reshape(1, d), w_out.astype(_BF16),
      ln_g.reshape(1, d), ln_b.reshape(1, d))


def _mlp_kernel(x_ref, wu_ref, wd_ref, g_ref, b_ref, o_ref, xb_ref, *, alpha):
    j = pl.program_id(1)

    @pl.when(j == 0)
    def _():
        xb_ref[...] = x_ref[...].astype(_BF16)
        o_ref[...] = jnp.zeros_like(o_ref)

    h = jnp.maximum(jnp.dot(xb_ref[...], wu_ref[...], preferred_element_type=_F32), 0.0)
    o_ref[...] += jnp.dot((h * h).astype(_BF16), wd_ref[...], preferred_element_type=_F32)

    @pl.when(j == pl.num_programs(1) - 1)
    def _():
        _residual_layer_norm_inplace(x_ref, o_ref, g_ref, b_ref, alpha)


def _mlp_layer(x, w_up, w_down, ln_g, ln_b, *, alpha):
    rows, d = x.shape
    d_ff = w_up.shape[1]
    tm = min(1024, rows)
    tf = min(512, d_ff)
    assert rows % tm == 0 and d_ff % tf == 0
    return pl.pallas_call(
        functools.partial(_mlp_kernel, alpha=alpha),
        out_shape=jax.ShapeDtypeStruct((rows, d), _F32),
        grid=(rows // tm, d_ff // tf),
        in_specs=[pl.BlockSpec((tm, d), lambda i, j: (i, 0)),
                  pl.BlockSpec((d, tf), lambda i, j: (0, j)),
                  pl.BlockSpec((tf, d), lambda i, j: (j, 0)),
                  _resident((1, d)), _resident((1, d))],
        out_specs=pl.BlockSpec((tm, d), lambda i, j: (i, 0)),
        scratch_shapes=[pltpu.VMEM((tm, d), _BF16)],
        compiler_params=_params("parallel", "arbitrary"),
        name="sqrelu_mlp",
    )(x, w_up.astype(_BF16), w_down.astype(_BF16), ln_g.reshape(1, d), ln_b.reshape(1, d))


def _qkv_kernel(x_ref, w_ref, o_ref, xb_ref):
    @pl.when(pl.program_id(1) == 0)
    def _():
        xb_ref[...] = x_ref[...].astype(_BF16)

    o_ref[...] = jnp.dot(xb_ref[...], w_ref[...], preferred_element_type=_F32).astype(o_ref.dtype)


def _attn_kernel(slope_ref, q_ref, *refs, dil, n_back, has_prev):
    if has_prev:
        kp_ref, kc_ref, vp_ref, vc_ref, o_ref, lse_ref = refs
    else:
        kc_ref, vc_ref, o_ref, lse_ref = refs
    nq = q_ref.shape[1]
    nk = 2 * nq if has_prev else nq
    n_heads = q_ref.shape[2] // HEAD_DIM
    qi = lax.broadcasted_iota(jnp.int32, (nq, nk), 0)
    ki = lax.broadcasted_iota(jnp.int32, (nq, nk), 1)
    dist = qi + (nk - nq) - ki
    valid = (dist >= 0) & (dist <= n_back)
    if has_prev:
        valid = valid & ((pl.program_id(2) > 0) | (ki >= nq))
    neg_dist = jnp.where(valid, -(dist * dil).astype(_F32), -MASK_DISTANCE)
    scale = 1.0 / math.sqrt(HEAD_DIM)
    for h in range(n_heads):
        hs = slice(h * HEAD_DIM, (h + 1) * HEAD_DIM)
        q = q_ref[0, :, hs]
        if has_prev:
            k = jnp.concatenate([kp_ref[0, :, hs], kc_ref[0, :, hs]], axis=0)
            v = jnp.concatenate([vp_ref[0, :, hs], vc_ref[0, :, hs]], axis=0)
        else:
            k = kc_ref[0, :, hs]
            v = vc_ref[0, :, hs]
        s = lax.dot_general(q, k, (((1,), (1,)), ((), ())), preferred_element_type=_F32)
        s = s * scale + slope_ref[h] * neg_dist
        m = jnp.max(s, axis=-1, keepdims=True)
        e = jnp.exp(s - m)
        den = jnp.sum(e, axis=-1, keepdims=True)
        o = jnp.dot(e.astype(_BF16), v, preferred_element_type=_F32) * (1.0 / den)
        o_ref[0, :, hs] = o.astype(o_ref.dtype)
        lse_ref[0, :, hs] = jnp.broadcast_to(m + jnp.log(den), (nq, HEAD_DIM))


def _attn_group(qkv, slopes, *, batch, seq, group, n_groups, gw):
    window, dil = ATTN_PATTERNS[group]
    n_back = window // dil
    assert n_back == ATTN_BLOCK
    sub_len = seq // dil
    assert sub_len % ATTN_BLOCK == 0
    nb = sub_len // ATTN_BLOCK
    has_prev = nb > 1
    cols_per_token = 3 * n_groups
    view = qkv.reshape(batch, sub_len, dil * cols_per_token * gw)

    def col_spec(part, prev):
        def index(b, r, blk):
            row = jnp.maximum(blk - 1, 0) if prev else blk
            return (b, row, r * cols_per_token + part * n_groups + group)
        return pl.BlockSpec((1, ATTN_BLOCK, gw), index)

    in_specs = [pl.BlockSpec(memory_space=pltpu.SMEM), col_spec(0, False)]
    operands = [slopes, view]
    for part in (1, 2):
        if has_prev:
            in_specs.append(col_spec(part, True))
            operands.append(view)
        in_specs.append(col_spec(part, False))
        operands.append(view)
    out_spec = pl.BlockSpec((1, ATTN_BLOCK, gw), lambda b, r, blk: (b, blk, r))
    o, lse = pl.pallas_call(
        functools.partial(_attn_kernel, dil=dil, n_back=n_back, has_prev=has_prev),
        out_shape=(jax.ShapeDtypeStruct((batch, sub_len, dil * gw), _BF16),
                   jax.ShapeDtypeStruct((batch, sub_len, dil * gw), _F32)),
        grid=(batch, dil, nb),
        in_specs=in_specs,
        out_specs=(out_spec, out_spec),
        compiler_params=_params("parallel", "parallel", "arbitrary"),
        name=f"dilated_attn_g{group}",
    )(*operands)
    return o.reshape(batch * seq, gw), lse.reshape(batch * seq, gw)


def _attn_out_kernel(*refs, alpha, n_groups):
    o_refs = refs[:n_groups]
    l_refs = refs[n_groups:2 * n_groups]
    x_ref, w_ref, g_ref, b_ref, out_ref, merged_ref = refs[2 * n_groups:]

    def body(c, carry):
        rows = pl.ds(pl.multiple_of(c * LN_ROW_CHUNK, LN_ROW_CHUNK), LN_ROW_CHUNK)
        lses = [l[rows, :] for l in l_refs]
        m = functools.reduce(jnp.maximum, lses)
        ws = [jnp.exp(l - m) for l in lses]
        num = functools.reduce(jnp.add, [w * o[rows, :].astype(_F32) for w, o in zip(ws, o_refs)])
        merged_ref[rows, :] = (num / functools.reduce(jnp.add, ws)).astype(merged_ref.dtype)
        return carry

    lax.fori_loop(0, x_ref.shape[0] // LN_ROW_CHUNK, body, 0)
    out_ref[...] = jnp.dot(merged_ref[...], w_ref[...], preferred_element_type=_F32)
    _residual_layer_norm_inplace(x_ref, out_ref, g_ref, b_ref, alpha)


def _alibi_slopes(n_groups, n_slots):
    n = n_groups * n_slots
    return jnp.exp2(-8.0 * jnp.arange(1, n + 1, dtype=_F32) / n).reshape(n_groups, n_slots)


def _attn_layer(x, w_qkv, w_out, ln_g, ln_b, *, batch, seq, alpha):
    rows, d = x.shape
    n_groups = len(ATTN_PATTERNS)
    gw = w_out.shape[0]
    n_slots = gw // HEAD_DIM
    assert w_qkv.shape[1] == 3 * n_groups * gw
    tm = min(1024, rows)
    qkv = pl.pallas_call(
        _qkv_kernel,
        out_shape=jax.ShapeDtypeStruct((rows, w_qkv.shape[1]), _BF16),
        grid=(rows // tm, w_qkv.shape[1] // gw),
        in_specs=[pl.BlockSpec((tm, d), lambda i, j: (i, 0)), pl.BlockSpec((d, gw), lambda i, j: (0, j))],
        out_specs=pl.BlockSpec((tm, gw), lambda i, j: (i, j)),
        scratch_shapes=[pltpu.VMEM((tm, d), _BF16)],
        compiler_params=_params("parallel", "arbitrary"),
        name="qkv_proj",
    )(x, w_qkv.astype(_BF16))
    slopes = _alibi_slopes(n_groups, n_slots)
    outs, lses = [], []
    for g in range(n_groups):
        o, lse = _attn_group(qkv, slopes[g], batch=batch, seq=seq, group=g, n_groups=n_groups, gw=gw)
        outs.append(o)
        lses.append(lse)
    tm = min(512, rows)
    head_block = pl.BlockSpec((tm, gw), lambda i: (i, 0))
    row_block = pl.BlockSpec((tm, d), lambda i: (i, 0))
    return pl.pallas_call(
        functools.partial(_attn_out_kernel, alpha=alpha, n_groups=n_groups),
        out_shape=jax.ShapeDtypeStruct((rows, d), _F32),
        grid=(rows // tm,),
        in_specs=[head_block] * (2 * n_groups) + [row_block, _resident((gw, d)), _resident((1, d)), _resident((1, d))],
        out_specs=row_block,
        scratch_shapes=[pltpu.VMEM((tm, gw), _BF16)],
        compiler_params=_params("parallel"),
        name="attn_merge_out",
    )(*outs, *lses, x, w_out.astype(_BF16), ln_g.reshape(1, d), ln_b.reshape(1, d))


def kernel(x, pool_w_in, pool_w_group, pool_scale, pool_w_out, attn_w_qkv, attn_w_out, mlp_w_up, mlp_w_down, ln_mix_g, ln_mix_b, ln_mlp_g, ln_mlp_b):
    batch, seq, d = x.shape
    depth = mlp_w_up.shape[0]
    alpha = (2 * depth) ** 0.25
    h = x.reshape(batch * seq, d)
    for i in range(depth):
        j = i // 2
        if i % 2 == 0:
            h = _pool_layer(h, pool_w_in[j], pool_w_group[j], pool_scale[j], pool_w_out[j],
                            ln_mix_g[i], ln_mix_b[i], seq=seq, alpha=alpha)
        else:
            h = _attn_layer(h, attn_w_qkv[j], attn_w_out[j], ln_mix_g[i], ln_mix_b[i],
                            batch=batch, seq=seq, alpha=alpha)
        h = _mlp_layer(h, mlp_w_up[i], mlp_w_down[i], ln_mlp_g[i], ln_mlp_b[i], alpha=alpha)
    return h.reshape(batch, seq, d)
```

```python
import functools
import math

import jax
import jax.numpy as jnp
from jax import lax
from jax.experimental import pallas as pl
from jax.experimental.pallas import tpu as pltpu

POOL_WINDOWS = (2, 4, 8, 16)
MAX_POOL_WINDOW = max(POOL_WINDOWS)
ATTN_PATTERNS = ((128, 1), (512, 4), (2048, 16))
HEAD_DIM = 128
ATTN_BLOCK = 128
LN_EPS = 1e-5
MASK_DISTANCE = 1e30
LOG2_E = math.log2(math.e)
LN_2 = math.log(2.0)

LANES = 128
V7X_VMEM_LIMIT_BYTES = 56 * 1024 * 1024
LN_ROW_CHUNK = 32
LN_UNROLL = 4
ATTN_HEADS_IN_FLIGHT = 8

_F32 = jnp.float32
_BF16 = jnp.bfloat16


def _params(*semantics):
    return pltpu.CompilerParams(dimension_semantics=semantics, vmem_limit_bytes=V7X_VMEM_LIMIT_BYTES)


def _resident(shape):
    return pl.BlockSpec(shape, lambda *_: (0,) * len(shape), pipeline_mode=pl.Buffered(1))


def _residual_layer_norm_inplace(x_ref, h_ref, g_ref, b_ref, alpha):
    gam = g_ref[...]
    bet = b_ref[...]
    group = LN_ROW_CHUNK * LN_UNROLL

    def step(c, carry):
        base = pl.multiple_of(c * group, group)
        chunks = [pl.ds(base + u * LN_ROW_CHUNK, LN_ROW_CHUNK) for u in range(LN_UNROLL)]
        rs = [alpha * x_ref[rows, :] + h_ref[rows, :] for rows in chunks]
        xcs = [r - jnp.mean(r, axis=-1, keepdims=True) for r in rs]
        rstds = [lax.rsqrt(jnp.mean(xc * xc, axis=-1, keepdims=True) + LN_EPS) for xc in xcs]
        for rows, xc, rstd in zip(chunks, xcs, rstds):
            h_ref[rows, :] = xc * rstd * gam + bet
        return carry

    lax.fori_loop(0, x_ref.shape[0] // group, step, 0)


def _pool_in_kernel(x_ref, w_ref, p_ref, carry_ref, *, blocks_per_seq):
    tm = x_ref.shape[0]
    gd = w_ref.shape[1] // len(POOL_WINDOWS)
    blk = pl.program_id(0) % blocks_per_seq

    @pl.when(blk == 0)
    def _():
        carry_ref[...] = jnp.zeros_like(carry_ref)

    xb = x_ref[...].astype(_BF16)
    pos = lax.broadcasted_iota(jnp.int32, (tm, 1), 0) + blk * tm
    for g, w in enumerate(POOL_WINDOWS):
        cols = slice(g * gd, (g + 1) * gd)
        u = jnp.dot(xb, w_ref[:, cols], preferred_element_type=_F32)
        t = jnp.concatenate([carry_ref[g], u], axis=0)
        carry_ref[g] = u[tm - MAX_POOL_WINDOW:, :]
        k = 1
        while k < w:
            t = t + pltpu.roll(t, k, 0)
            k *= 2
        inv_cnt = 1.0 / jnp.minimum(pos + 1, w).astype(_F32)
        p_ref[:, cols] = (t[MAX_POOL_WINDOW:, :] * inv_cnt - u).astype(p_ref.dtype)


def _pool_out_kernel(p_ref, x_ref, wg_ref, scale_ref, wo_ref, g_ref, b_ref, o_ref, *, alpha):
    n_groups, gd, _ = wg_ref.shape
    ys = []
    for g in range(n_groups):
        cols = slice(g * gd, (g + 1) * gd)
        y = jnp.dot(p_ref[:, cols], wg_ref[g], preferred_element_type=_F32) * scale_ref[:, cols]
        ys.append(y.astype(_BF16))
    yb = jnp.concatenate(ys, axis=1)
    o_ref[...] = jnp.dot(yb, wo_ref[...], preferred_element_type=_F32)
    _residual_layer_norm_inplace(x_ref, o_ref, g_ref, b_ref, alpha)


def _pool_layer(x, w_in, w_group, scale, w_out, ln_g, ln_b, *, seq, alpha):
    rows, d = x.shape
    n_groups, gd, _ = w_group.shape
    tm = min(512, seq)
    assert seq % tm == 0 and tm >= MAX_POOL_WINDOW and gd * n_groups == d
    row_block = pl.BlockSpec((tm, d), lambda i: (i, 0))
    p = pl.pallas_call(
        functools.partial(_pool_in_kernel, blocks_per_seq=seq // tm),
        out_shape=jax.ShapeDtypeStruct((rows, d), _BF16),
        grid=(rows // tm,),
        in_specs=[row_block, _resident((d, d))],
        out_specs=row_block,
        scratch_shapes=[pltpu.VMEM((n_groups, MAX_POOL_WINDOW, gd), _F32)],
        compiler_params=_params("arbitrary"),
        name="pool_in",
    )(x, w_in.astype(_BF16))
    return pl.pallas_call(
        functools.partial(_pool_out_kernel, alpha=alpha),
        out_shape=jax.ShapeDtypeStruct((rows, d), _F32),
        grid=(rows // tm,),
        in_specs=[row_block, row_block, _resident((n_groups, gd, gd)), _resident((1, d)),
                  _resident((d, d)), _resident((1, d)), _resident((1, d))],
        out_specs=row_block,
        compiler_params=_params("parallel"),
        name="pool_out",
    )(p, x, w_group.astype(_BF16), scale.reshape(1, d), w_out.astype(_BF16),
      ln_g.reshape(1, d), ln_b.reshape(1, d))


def _mlp_kernel(x_ref, wu_ref, wd_ref, g_ref, b_ref, o_ref, xb_ref, *, alpha):
    j = pl.program_id(1)

    @pl.when(j == 0)
    def _():
        xb_ref[...] = x_ref[...].astype(_BF16)
        o_ref[...] = jnp.zeros_like(o_ref)

    h = jnp.maximum(jnp.dot(xb_ref[...], wu_ref[...], preferred_element_type=_F32), 0.0)
    o_ref[...] += jnp.dot((h * h).astype(_BF16), wd_ref[...], preferred_element_type=_F32)

    @pl.when(j == pl.num_programs(1) - 1)
    def _():
        _residual_layer_norm_inplace(x_ref, o_ref, g_ref, b_ref, alpha)


def _mlp_layer(x, w_up, w_down, ln_g, ln_b, *, alpha):
    rows, d = x.shape
    d_ff = w_up.shape[1]
    tm = min(1024, rows)
    tf = min(512, d_ff)
    assert rows % tm == 0 and d_ff % tf == 0
    return pl.pallas_call(
        functools.partial(_mlp_kernel, alpha=alpha),
        out_shape=jax.ShapeDtypeStruct((rows, d), _F32),
        grid=(rows // tm, d_ff // tf),
        in_specs=[pl.BlockSpec((tm, d), lambda i, j: (i, 0)),
                  pl.BlockSpec((d, tf), lambda i, j: (0, j)),
                  pl.BlockSpec((tf, d), lambda i, j: (j, 0)),
                  _resident((1, d)), _resident((1, d))],
        out_specs=pl.BlockSpec((tm, d), lambda i, j: (i, 0)),
        scratch_shapes=[pltpu.VMEM((tm, d), _BF16)],
        compiler_params=_params("parallel", "arbitrary"),
        name="sqrelu_mlp",
    )(x, w_up.astype(_BF16), w_down.astype(_BF16), ln_g.reshape(1, d), ln_b.reshape(1, d))


def _qkv_kernel(x_ref, w_ref, o_ref, xb_ref, *scratch, dil):
    @pl.when(pl.program_id(1) == 0)
    def _():
        xb_ref[...] = x_ref[...].astype(_BF16)

    res = jnp.dot(xb_ref[...], w_ref[...], preferred_element_type=_F32)
    if dil == 1:
        o_ref[0] = res.astype(o_ref.dtype)
        return
    (res_ref,) = scratch
    n = o_ref.shape[1]
    for c in range(res_ref.shape[0]):
        res_ref[c] = res[:, c * LANES:(c + 1) * LANES]
    for r in range(dil):
        for c in range(res_ref.shape[0]):
            o_ref[r, :, c * LANES:(c + 1) * LANES] = res_ref[c, pl.ds(r, n, stride=dil), :].astype(o_ref.dtype)


def _qkv_group(x, w_qkv, *, batch, seq, group, n_groups, gw):
    rows, d = x.shape
    _, dil = ATTN_PATTERNS[group]
    tm = min(1024, seq)
    assert seq % tm == 0 and tm % (dil * 16) == 0
    blocks_per_seq = seq // tm
    scratch = [pltpu.VMEM((tm, d), _BF16)]
    if dil > 1:
        scratch.append(pltpu.VMEM((gw // LANES, tm, LANES), _F32))
    return pl.pallas_call(
        functools.partial(_qkv_kernel, dil=dil),
        out_shape=jax.ShapeDtypeStruct((batch, dil, seq // dil, 3 * gw), _BF16),
        grid=(rows // tm, 3),
        in_specs=[pl.BlockSpec((tm, d), lambda i, j: (i, 0)),
                  pl.BlockSpec((d, gw), lambda i, j: (0, j * n_groups + group))],
        out_specs=pl.BlockSpec((None, dil, tm // dil, gw),
                               lambda i, j: (i // blocks_per_seq, 0, i % blocks_per_seq, j)),
        scratch_shapes=scratch,
        compiler_params=_params("parallel", "arbitrary"),
        name=f"qkv_proj_g{group}",
    )(x, w_qkv)


def _attn_kernel(slope_ref, q_ref, k_ref, v_ref, o_ref, lse_ref, osc_ref, *, dil, n_back, heads_in_flight):
    sub_len = q_ref.shape[1]
    n_heads = q_ref.shape[2] // HEAD_DIM
    nb = sub_len // ATTN_BLOCK
    nq = ATTN_BLOCK
    nk = 2 * nq if nb > 1 else nq
    q_minus_k = lax.broadcasted_iota(jnp.int32, (nq, nk), 0) - lax.broadcasted_iota(jnp.int32, (nq, nk), 1)
    lane = lax.broadcasted_iota(jnp.int32, (nq, LANES), 1)
    qk_scale = LOG2_E / math.sqrt(HEAD_DIM)

    def block(p, carry):
        r = p // nb
        q0 = pl.multiple_of((p % nb) * nq, nq)
        k0 = pl.multiple_of(jnp.maximum(q0 - nq, 0), nq) if nb > 1 else 0
        dist = q_minus_k + (q0 - k0)
        valid = (dist >= 0) & (dist <= n_back)
        neg_dist = jnp.where(valid, -(dist * dil).astype(_F32), -MASK_DISTANCE)
        tok = pl.ds(q0 * dil + r, nq, stride=dil) if dil > 1 else pl.ds(q0, nq)
        lse = jnp.zeros((nq, LANES), _F32)
        for h0 in range(0, n_heads, heads_in_flight):
            heads = range(h0, min(h0 + heads_in_flight, n_heads))
            cols = [slice(h * HEAD_DIM, (h + 1) * HEAD_DIM) for h in heads]
            ss = [lax.dot_general(q_ref[r, pl.ds(q0, nq), c], k_ref[r, pl.ds(k0, nk), c],
                                  (((1,), (1,)), ((), ())), preferred_element_type=_F32) for c in cols]
            ss = [s * qk_scale + (slope_ref[h] * LOG2_E) * neg_dist for s, h in zip(ss, heads)]
            ms = [jnp.max(s, axis=-1, keepdims=True) for s in ss]
            es = [jnp.exp2(s - m) for s, m in zip(ss, ms)]
            dens = [jnp.sum(e, axis=-1, keepdims=True) for e in es]
            os = [jnp.dot(e.astype(_BF16), v_ref[r, pl.ds(k0, nk), c], preferred_element_type=_F32)
                  for e, c in zip(es, cols)]
            for h, o, m, den in zip(heads, os, ms, dens):
                osc_ref.at[h][tok, :] = o * (1.0 / den)
                lse = jnp.where(lane == h, (m + jnp.log2(den)) * LN_2, lse)
        lse_ref[tok, :] = lse
        return carry

    lax.fori_loop(0, dil * nb, block, 0)
    for h in range(n_heads):
        o_ref[:, h * HEAD_DIM:(h + 1) * HEAD_DIM] = osc_ref[h].astype(o_ref.dtype)


def _attn_group(qkv, slopes, *, batch, seq, group, gw):
    window, dil = ATTN_PATTERNS[group]
    sub_len = seq // dil
    assert window // dil == ATTN_BLOCK and sub_len % ATTN_BLOCK == 0 and gw // HEAD_DIM <= LANES

    def part(j):
        return pl.BlockSpec((None, dil, sub_len, gw), lambda b: (b, 0, 0, j))

    return pl.pallas_call(
        functools.partial(_attn_kernel, dil=dil, n_back=window // dil, heads_in_flight=ATTN_HEADS_IN_FLIGHT),
        out_shape=(jax.ShapeDtypeStruct((batch * seq, gw), _BF16),
                   jax.ShapeDtypeStruct((batch * seq, LANES), _F32)),
        grid=(batch,),
        in_specs=[pl.BlockSpec(memory_space=pltpu.SMEM), part(0), part(1), part(2)],
        out_specs=(pl.BlockSpec((seq, gw), lambda b: (b, 0)), pl.BlockSpec((seq, LANES), lambda b: (b, 0))),
        scratch_shapes=[pltpu.VMEM((gw // HEAD_DIM, seq, HEAD_DIM), _F32)],
        compiler_params=_params("parallel"),
        name=f"dilated_attn_g{group}",
    )(slopes, qkv, qkv, qkv)


def _attn_out_kernel(*refs, alpha, n_groups):
    o_refs = refs[:n_groups]
    l_refs = refs[n_groups:2 * n_groups]
    x_ref, w_ref, g_ref, b_ref, out_ref = refs[2 * n_groups:]
    gw = w_ref.shape[0]
    expand = (lax.broadcasted_iota(jnp.int32, (2 * LANES, gw), 1) // HEAD_DIM
              == lax.broadcasted_iota(jnp.int32, (2 * LANES, gw), 0) % LANES).astype(_BF16)

    lses = [l[...] for l in l_refs]
    m = functools.reduce(jnp.maximum, lses)
    es = [jnp.exp(l - m) for l in lses]
    inv = 1.0 / functools.reduce(jnp.add, es)
    acc = None
    for e, o_ref in zip(es, o_refs):
        w = e * inv
        w_hi = w.astype(_BF16)
        w_lo = (w - w_hi.astype(_F32)).astype(_BF16)
        wide = jnp.dot(jnp.concatenate([w_hi, w_lo], axis=1), expand, preferred_element_type=_F32)
        term = wide * o_ref[...].astype(_F32)
        acc = term if acc is None else acc + term
    out_ref[...] = jnp.dot(acc.astype(_BF16), w_ref[...], preferred_element_type=_F32)
    _residual_layer_norm_inplace(x_ref, out_ref, g_ref, b_ref, alpha)


def _alibi_slopes(n_groups, n_slots):
    n = n_groups * n_slots
    return jnp.exp2(-8.0 * jnp.arange(1, n + 1, dtype=_F32) / n).reshape(n_groups, n_slots)


def _attn_layer(x, w_qkv, w_out, ln_g, ln_b, *, batch, seq, alpha):
    rows, d = x.shape
    n_groups = len(ATTN_PATTERNS)
    gw = w_out.shape[0]
    assert w_qkv.shape[1] == 3 * n_groups * gw
    w_qkv = w_qkv.astype(_BF16)
    slopes = _alibi_slopes(n_groups, gw // HEAD_DIM)
    outs, lses = [], []
    for g in range(n_groups):
        qkv = _qkv_group(x, w_qkv, batch=batch, seq=seq, group=g, n_groups=n_groups, gw=gw)
        o, lse = _attn_group(qkv, slopes[g], batch=batch, seq=seq, group=g, gw=gw)
        outs.append(o)
        lses.append(lse)
    tm = min(512, rows)
    head_block = pl.BlockSpec((tm, gw), lambda i: (i, 0))
    lse_block = pl.BlockSpec((tm, LANES), lambda i: (i, 0))
    row_block = pl.BlockSpec((tm, d), lambda i: (i, 0))
    return pl.pallas_call(
        functools.partial(_attn_out_kernel, alpha=alpha, n_groups=n_groups),
        out_shape=jax.ShapeDtypeStruct((rows, d), _F32),
        grid=(rows // tm,),
        in_specs=[head_block] * n_groups + [lse_block] * n_groups
        + [row_block, _resident((gw, d)), _resident((1, d)), _resident((1, d))],
        out_specs=row_block,
        compiler_params=_params("parallel"),
        name="attn_merge_out",
    )(*outs, *lses, x, w_out.astype(_BF16), ln_g.reshape(1, d), ln_b.reshape(1, d))


def kernel(x, pool_w_in, pool_w_group, pool_scale, pool_w_out, attn_w_qkv, attn_w_out, mlp_w_up, mlp_w_down, ln_mix_g, ln_mix_b, ln_mlp_g, ln_mlp_b):
    batch, seq, d = x.shape
    depth = mlp_w_up.shape[0]
    alpha = (2 * depth) ** 0.25
    h = x.reshape(batch * seq, d)
    for i in range(depth):
        j = i // 2
        if i % 2 == 0:
            h = _pool_layer(h, pool_w_in[j], pool_w_group[j], pool_scale[j], pool_w_out[j],
                            ln_mix_g[i], ln_mix_b[i], seq=seq, alpha=alpha)
        else:
            h = _attn_layer(h, attn_w_qkv[j], attn_w_out[j], ln_mix_g[i], ln_mix_b[i],
                            batch=batch, seq=seq, alpha=alpha)
        h = _mlp_layer(h, mlp_w_up[i], mlp_w_down[i], ln_mlp_g[i], ln_mlp_b[i], alpha=alpha)
    return h.reshape(batch, seq, d)
```

```python
import functools
import math

import jax
import jax.numpy as jnp
from jax import lax
from jax.experimental import pallas as pl
from jax.experimental.pallas import tpu as pltpu

POOL_WINDOWS = (2, 4, 8, 16)
MAX_POOL_WINDOW = max(POOL_WINDOWS)
ATTN_PATTERNS = ((128, 1), (512, 4), (2048, 16))
HEAD_DIM = 128
ATTN_BLOCK = 128
LN_EPS = 1e-5
MASK_DISTANCE = 1e30
LOG2_E = math.log2(math.e)
LN_2 = math.log(2.0)

LANES = 128
V7X_VMEM_LIMIT_BYTES = 60 * 1024 * 1024
LN_ROW_CHUNK = 32
LN_UNROLL = 4
ATTN_BLOCKS_IN_FLIGHT = 1
MLP_FF_BLOCK = 1024
MLP_SUB_CHUNK = 512
QKV_MAX_STRIDE = 4

_F32 = jnp.float32
_BF16 = jnp.bfloat16


def _params(*semantics):
    return pltpu.CompilerParams(dimension_semantics=semantics, vmem_limit_bytes=V7X_VMEM_LIMIT_BYTES)


def _resident(shape):
    return pl.BlockSpec(shape, lambda *_: (0,) * len(shape), pipeline_mode=pl.Buffered(1))


def _residual_layer_norm_inplace(x_ref, h_ref, g_ref, b_ref, alpha):
    gam = g_ref[...]
    bet = b_ref[...]
    group = LN_ROW_CHUNK * LN_UNROLL

    def step(c, carry):
        base = pl.multiple_of(c * group, group)
        chunks = [pl.ds(base + u * LN_ROW_CHUNK, LN_ROW_CHUNK) for u in range(LN_UNROLL)]
        rs = [alpha * x_ref[rows, :] + h_ref[rows, :] for rows in chunks]
        xcs = [r - jnp.mean(r, axis=-1, keepdims=True) for r in rs]
        rstds = [lax.rsqrt(jnp.mean(xc * xc, axis=-1, keepdims=True) + LN_EPS) for xc in xcs]
        for rows, xc, rstd in zip(chunks, xcs, rstds):
            h_ref[rows, :] = xc * rstd * gam + bet
        return carry

    lax.fori_loop(0, x_ref.shape[0] // group, step, 0)


def _pool_in_kernel(x_ref, w_ref, p_ref, carry_ref, *, blocks_per_seq):
    tm = x_ref.shape[0]
    gd = w_ref.shape[1] // len(POOL_WINDOWS)
    blk = pl.program_id(0) % blocks_per_seq

    @pl.when(blk == 0)
    def _():
        carry_ref[...] = jnp.zeros_like(carry_ref)

    xb = x_ref[...].astype(_BF16)
    pos = lax.broadcasted_iota(jnp.int32, (tm, 1), 0) + blk * tm
    for g, w in enumerate(POOL_WINDOWS):
        cols = slice(g * gd, (g + 1) * gd)
        u = jnp.dot(xb, w_ref[:, cols], preferred_element_type=_F32)
        t = jnp.concatenate([carry_ref[g], u], axis=0)
        carry_ref[g] = u[tm - MAX_POOL_WINDOW:, :]
        k = 1
        while k < w:
            t = t + pltpu.roll(t, k, 0)
            k *= 2
        inv_cnt = 1.0 / jnp.minimum(pos + 1, w).astype(_F32)
        p_ref[:, cols] = (t[MAX_POOL_WINDOW:, :] * inv_cnt - u).astype(p_ref.dtype)


def _pool_out_kernel(p_ref, x_ref, wg_ref, scale_ref, wo_ref, g_ref, b_ref, o_ref, *, alpha):
    n_groups, gd, _ = wg_ref.shape
    ys = []
    for g in range(n_groups):
        cols = slice(g * gd, (g + 1) * gd)
        y = jnp.dot(p_ref[:, cols], wg_ref[g], preferred_element_type=_F32) * scale_ref[:, cols]
        ys.append(y.astype(_BF16))
    yb = jnp.concatenate(ys, axis=1)
    o_ref[...] = jnp.dot(yb, wo_ref[...], preferred_element_type=_F32)
    _residual_layer_norm_inplace(x_ref, o_ref, g_ref, b_ref, alpha)


def _pool_layer(x, w_in, w_group, scale, w_out, ln_g, ln_b, *, seq, alpha):
    rows, d = x.shape
    n_groups, gd, _ = w_group.shape
    tm = min(512, seq)
    assert seq % tm == 0 and tm >= MAX_POOL_WINDOW and gd * n_groups == d
    row_block = pl.BlockSpec((tm, d), lambda i: (i, 0))
    p = pl.pallas_call(
        functools.partial(_pool_in_kernel, blocks_per_seq=seq // tm),
        out_shape=jax.ShapeDtypeStruct((rows, d), _BF16),
        grid=(rows // tm,),
        in_specs=[row_block, _resident((d, d))],
        out_specs=row_block,
        scratch_shapes=[pltpu.VMEM((n_groups, MAX_POOL_WINDOW, gd), _F32)],
        compiler_params=_params("arbitrary"),
        name="pool_in",
    )(x, w_in.astype(_BF16))
    return pl.pallas_call(
        functools.partial(_pool_out_kernel, alpha=alpha),
        out_shape=jax.ShapeDtypeStruct((rows, d), _F32),
        grid=(rows // tm,),
        in_specs=[row_block, row_block, _resident((n_groups, gd, gd)), _resident((1, d)),
                  _resident((d, d)), _resident((1, d)), _resident((1, d))],
        out_specs=row_block,
        compiler_params=_params("parallel"),
        name="pool_out",
    )(p, x, w_group.astype(_BF16), scale.reshape(1, d), w_out.astype(_BF16),
      ln_g.reshape(1, d), ln_b.reshape(1, d))


def _mlp_kernel(x_ref, wu_ref, wd_ref, g_ref, b_ref, o_ref, xb_ref, *, alpha):
    j = pl.program_id(1)

    @pl.when(j == 0)
    def _():
        xb_ref[...] = x_ref[...].astype(_BF16)
        o_ref[...] = jnp.zeros_like(o_ref)

    for c in range(wu_ref.shape[1] // MLP_SUB_CHUNK):
        cols = slice(c * MLP_SUB_CHUNK, (c + 1) * MLP_SUB_CHUNK)
        h = jnp.maximum(jnp.dot(xb_ref[...], wu_ref[:, cols], preferred_element_type=_F32), 0.0)
        o_ref[...] += jnp.dot((h * h).astype(_BF16), wd_ref[cols, :], preferred_element_type=_F32)

    @pl.when(j == pl.num_programs(1) - 1)
    def _():
        _residual_layer_norm_inplace(x_ref, o_ref, g_ref, b_ref, alpha)


def _mlp_layer(x, w_up, w_down, ln_g, ln_b, *, alpha):
    rows, d = x.shape
    d_ff = w_up.shape[1]
    tm = min(1024, rows)
    tf = min(MLP_FF_BLOCK, d_ff)
    assert rows % tm == 0 and d_ff % tf == 0 and tf % MLP_SUB_CHUNK == 0
    return pl.pallas_call(
        functools.partial(_mlp_kernel, alpha=alpha),
        out_shape=jax.ShapeDtypeStruct((rows, d), _F32),
        grid=(rows // tm, d_ff // tf),
        in_specs=[pl.BlockSpec((tm, d), lambda i, j: (i, 0)),
                  pl.BlockSpec((d, tf), lambda i, j: (0, j)),
                  pl.BlockSpec((tf, d), lambda i, j: (j, 0)),
                  _resident((1, d)), _resident((1, d))],
        out_specs=pl.BlockSpec((tm, d), lambda i, j: (i, 0)),
        scratch_shapes=[pltpu.VMEM((tm, d), _BF16)],
        compiler_params=_params("parallel", "arbitrary"),
        name="sqrelu_mlp",
    )(x, w_up.astype(_BF16), w_down.astype(_BF16), ln_g.reshape(1, d), ln_b.reshape(1, d))


def _qkv_kernel(x_ref, w_ref, o_ref, xb_ref, *scratch, dil):
    @pl.when(pl.program_id(1) == 0)
    def _():
        xb_ref[...] = x_ref[...].astype(_BF16)

    res = jnp.dot(xb_ref[...], w_ref[...], preferred_element_type=_F32)
    if dil == 1:
        o_ref[0] = res.astype(o_ref.dtype)
        return
    res_ref = scratch[0]
    tm = res_ref.shape[1]
    n = tm // dil
    col_tiles = [slice(c * LANES, (c + 1) * LANES) for c in range(res_ref.shape[0])]
    for c, cols in enumerate(col_tiles):
        res_ref[c] = res[:, cols]
    if dil <= QKV_MAX_STRIDE:
        for r in range(dil):
            for c, cols in enumerate(col_tiles):
                o_ref[r, :, cols] = res_ref[c, pl.ds(r, n, stride=dil), :].astype(o_ref.dtype)
        return
    tmp_ref = scratch[1]
    s1 = QKV_MAX_STRIDE
    s2 = dil // s1
    for b in range(s1):
        for c in range(len(col_tiles)):
            tmp_ref[c, b * (tm // s1):(b + 1) * (tm // s1), :] = res_ref[c, pl.ds(b, tm // s1, stride=s1), :]
    for r in range(dil):
        for c, cols in enumerate(col_tiles):
            src = pl.ds((r % s1) * (tm // s1) + r // s1, n, stride=s2)
            o_ref[r, :, cols] = tmp_ref[c, src, :].astype(o_ref.dtype)


def _qkv_group(x, w_qkv, *, batch, seq, group, n_groups, gw):
    rows, d = x.shape
    _, dil = ATTN_PATTERNS[group]
    tm = min(1024, seq)
    assert seq % tm == 0 and tm % (dil * 16) == 0
    assert dil <= QKV_MAX_STRIDE or dil % QKV_MAX_STRIDE == 0
    blocks_per_seq = seq // tm
    scratch = [pltpu.VMEM((tm, d), _BF16)]
    scratch += [pltpu.VMEM((gw // LANES, tm, LANES), _F32)] * ((dil > 1) + (dil > QKV_MAX_STRIDE))
    return pl.pallas_call(
        functools.partial(_qkv_kernel, dil=dil),
        out_shape=jax.ShapeDtypeStruct((batch, 3, dil, seq // dil, gw), _BF16),
        grid=(rows // tm, 3),
        in_specs=[pl.BlockSpec((tm, d), lambda i, j: (i, 0)),
                  pl.BlockSpec((d, gw), lambda i, j: (0, j * n_groups + group))],
        out_specs=pl.BlockSpec((None, None, dil, tm // dil, gw),
                               lambda i, j: (i // blocks_per_seq, j, 0, i % blocks_per_seq, 0)),
        scratch_shapes=scratch,
        compiler_params=_params("parallel", "arbitrary"),
        name=f"qkv_proj_g{group}",
    )(x, w_qkv)


def _attn_kernel(slope_ref, q_ref, k_ref, v_ref, o_ref, lse_ref, osc_ref, bias_ref, *, dil, n_back, blocks_in_flight):
    sub_len = q_ref.shape[1]
    n_heads = q_ref.shape[2] // HEAD_DIM
    nb = sub_len // ATTN_BLOCK
    nq = ATTN_BLOCK
    nk = bias_ref.shape[-1]
    lane = lax.broadcasted_iota(jnp.int32, (nq, LANES), 1)
    qk_scale = LOG2_E / math.sqrt(HEAD_DIM)

    @pl.when(pl.program_id(0) == 0)
    def _():
        q_minus_k = lax.broadcasted_iota(jnp.int32, (nq, nk), 0) - lax.broadcasted_iota(jnp.int32, (nq, nk), 1)
        for i in range(bias_ref.shape[0]):
            dist = q_minus_k + i * nq
            valid = (dist >= 0) & (dist <= n_back)
            neg_dist = jnp.where(valid, -(dist * dil).astype(_F32), -MASK_DISTANCE)
            for h in range(n_heads):
                bias_ref[i, h] = (slope_ref[h] * LOG2_E) * neg_dist

    def blocks(i, carry):
        work = []
        for u in range(blocks_in_flight):
            p = i * blocks_in_flight + u
            r = p // nb
            q0 = pl.multiple_of((p % nb) * nq, nq)
            k0 = pl.multiple_of(jnp.maximum(q0 - nq, 0), nq) if nb > 1 else 0
            tok = pl.ds(q0 * dil + r, nq, stride=dil) if dil > 1 else pl.ds(q0, nq)
            work.append((r, pl.ds(q0, nq), pl.ds(k0, nk), (q0 - k0) // nq, tok))
        items = [(w, h) for w in work for h in range(n_heads)]
        cols = [slice(h * HEAD_DIM, (h + 1) * HEAD_DIM) for h in range(n_heads)]
        ss = [lax.dot_general(q_ref[r, qs, cols[h]], k_ref[r, ks, cols[h]], (((1,), (1,)), ((), ())),
                              preferred_element_type=_F32) * qk_scale + bias_ref[back, h]
              for (r, qs, ks, back, _), h in items]
        ms = [jnp.max(s, axis=-1, keepdims=True) for s in ss]
        es = [jnp.exp2(s - m) for s, m in zip(ss, ms)]
        dens = [jnp.sum(e, axis=-1, keepdims=True) for e in es]
        os = [jnp.dot(e.astype(_BF16), v_ref[r, ks, cols[h]], preferred_element_type=_F32)
              for e, ((r, _, ks, _, _), h) in zip(es, items)]
        for u, w in enumerate(work):
            tok = w[4]
            m_all = jnp.zeros((nq, LANES), _F32)
            den_all = jnp.ones((nq, LANES), _F32)
            for h in range(n_heads):
                j = u * n_heads + h
                osc_ref.at[h][tok, :] = os[j] * (1.0 / dens[j])
                m_all = jnp.where(lane == h, ms[j], m_all)
                den_all = jnp.where(lane == h, dens[j], den_all)
            lse_ref[tok, :] = (m_all + jnp.log2(den_all)) * LN_2
        return carry

    lax.fori_loop(0, dil * nb // blocks_in_flight, blocks, 0)
    for h in range(n_heads):
        o_ref[:, h * HEAD_DIM:(h + 1) * HEAD_DIM] = osc_ref[h].astype(o_ref.dtype)


def _attn_group(qkv, slopes, *, batch, seq, group, gw):
    window, dil = ATTN_PATTERNS[group]
    sub_len = seq // dil
    assert window // dil == ATTN_BLOCK and sub_len % ATTN_BLOCK == 0 and gw // HEAD_DIM <= LANES
    n_heads = gw // HEAD_DIM
    windows = 2 if sub_len > ATTN_BLOCK else 1

    def part(j):
        return pl.BlockSpec((None, None, dil, sub_len, gw), lambda b: (b, j, 0, 0, 0))

    return pl.pallas_call(
        functools.partial(_attn_kernel, dil=dil, n_back=window // dil, blocks_in_flight=ATTN_BLOCKS_IN_FLIGHT),
        out_shape=(jax.ShapeDtypeStruct((batch * seq, gw), _BF16),
                   jax.ShapeDtypeStruct((batch * seq, LANES), _F32)),
        grid=(batch,),
        in_specs=[pl.BlockSpec(memory_space=pltpu.SMEM), part(0), part(1), part(2)],
        out_specs=(pl.BlockSpec((seq, gw), lambda b: (b, 0)), pl.BlockSpec((seq, LANES), lambda b: (b, 0))),
        scratch_shapes=[pltpu.VMEM((n_heads, seq, HEAD_DIM), _F32),
                        pltpu.VMEM((windows, n_heads, ATTN_BLOCK, windows * ATTN_BLOCK), _F32)],
        compiler_params=_params("arbitrary"),
        name=f"dilated_attn_g{group}",
    )(slopes, qkv, qkv, qkv)


def _attn_out_kernel(*refs, alpha, n_groups):
    o_refs = refs[:n_groups]
    l_refs = refs[n_groups:2 * n_groups]
    x_ref, w_ref, g_ref, b_ref, out_ref = refs[2 * n_groups:]
    gw = w_ref.shape[0]
    expand = (lax.broadcasted_iota(jnp.int32, (2 * LANES, gw), 1) // HEAD_DIM
              == lax.broadcasted_iota(jnp.int32, (2 * LANES, gw), 0) % LANES).astype(_BF16)

    lses = [l[...] for l in l_refs]
    m = functools.reduce(jnp.maximum, lses)
    es = [jnp.exp(l - m) for l in lses]
    inv = 1.0 / functools.reduce(jnp.add, es)
    acc = None
    for e, o_ref in zip(es, o_refs):
        w = e * inv
        w_hi = w.astype(_BF16)
        w_lo = (w - w_hi.astype(_F32)).astype(_BF16)
        wide = jnp.dot(jnp.concatenate([w_hi, w_lo], axis=1), expand, preferred_element_type=_F32)
        term = wide * o_ref[...].astype(_F32)
        acc = term if acc is None else acc + term
    out_ref[...] = jnp.dot(acc.astype(_BF16), w_ref[...], preferred_element_type=_F32)
    _residual_layer_norm_inplace(x_ref, out_ref, g_ref, b_ref, alpha)


def _alibi_slopes(n_groups, n_slots):
    n = n_groups * n_slots
    return jnp.exp2(-8.0 * jnp.arange(1, n + 1, dtype=_F32) / n).reshape(n_groups, n_slots)


def _attn_layer(x, w_qkv, w_out, ln_g, ln_b, *, batch, seq, alpha):
    rows, d = x.shape
    n_groups = len(ATTN_PATTERNS)
    gw = w_out.shape[0]
    assert w_qkv.shape[1] == 3 * n_groups * gw
    w_qkv = w_qkv.astype(_BF16)
    slopes = _alibi_slopes(n_groups, gw // HEAD_DIM)
    outs, lses = [], []
    for g in range(n_groups):
        qkv = _qkv_group(x, w_qkv, batch=batch, seq=seq, group=g, n_groups=n_groups, gw=gw)
        o, lse = _attn_group(qkv, slopes[g], batch=batch, seq=seq, group=g, gw=gw)
        outs.append(o)
        lses.append(lse)
    tm = min(512, rows)
    head_block = pl.BlockSpec((tm, gw), lambda i: (i, 0))
    lse_block = pl.BlockSpec((tm, LANES), lambda i: (i, 0))
    row_block = pl.BlockSpec((tm, d), lambda i: (i, 0))
    return pl.pallas_call(
        functools.partial(_attn_out_kernel, alpha=alpha, n_groups=n_groups),
        out_shape=jax.ShapeDtypeStruct((rows, d), _F32),
        grid=(rows // tm,),
        in_specs=[head_block] * n_groups + [lse_block] * n_groups
        + [row_block, _resident((gw, d)), _resident((1, d)), _resident((1, d))],
        out_specs=row_block,
        compiler_params=_params("parallel"),
        name="attn_merge_out",
    )(*outs, *lses, x, w_out.astype(_BF16), ln_g.reshape(1, d), ln_b.reshape(1, d))


def kernel(x, pool_w_in, pool_w_group, pool_scale, pool_w_out, attn_w_qkv, attn_w_out, mlp_w_up, mlp_w_down, ln_mix_g, ln_mix_b, ln_mlp_g, ln_mlp_b):
    batch, seq, d = x.shape
    depth = mlp_w_up.shape[0]
    alpha = (2 * depth) ** 0.25
    h = x.reshape(batch * seq, d)
    for i in range(depth):
        j = i // 2
        if i % 2 == 0:
            h = _pool_layer(h, pool_w_in[j], pool_w_group[j], pool_scale[j], pool_w_out[j],
                            ln_mix_g[i], ln_mix_b[i], seq=seq, alpha=alpha)
        else:
            h = _attn_layer(h, attn_w_qkv[j], attn_w_out[j], ln_mix_g[i], ln_mix_b[i],
                            batch=batch, seq=seq, alpha=alpha)
        h = _mlp_layer(h, mlp_w_up[i], mlp_w_down[i], ln_mlp_g[i], ln_mlp_b[i], alpha=alpha)
    return h.reshape(batch, seq, d)
```

```python
import functools
import math

import jax
import jax.numpy as jnp
from jax import lax
from jax.experimental import pallas as pl
from jax.experimental.pallas import tpu as pltpu

POOL_WINDOWS = (2, 4, 8, 16)
MAX_POOL_WINDOW = max(POOL_WINDOWS)
ATTN_PATTERNS = ((128, 1), (512, 4), (2048, 16))
HEAD_DIM = 128
ATTN_BLOCK = 128
LN_EPS = 1e-5
MASK_DISTANCE = 1e30
LOG2_E = math.log2(math.e)
LN_2 = math.log(2.0)

LANES = 128
BF16_SUBLANES = 16
V7X_VMEM_LIMIT_BYTES = 60 * 1024 * 1024
LN_ROW_CHUNK = 32
LN_UNROLL = 4
EPILOGUE_SLABS = 4
MLP_FF_BLOCK = 1024
MLP_SUB_CHUNK = 512
QKV_MAX_STRIDE = 4

_F32 = jnp.float32
_BF16 = jnp.bfloat16


def _params(*semantics):
    return pltpu.CompilerParams(dimension_semantics=semantics, vmem_limit_bytes=V7X_VMEM_LIMIT_BYTES)


def _resident(shape):
    return pl.BlockSpec(shape, lambda *_: (0,) * len(shape), pipeline_mode=pl.Buffered(1))


def _residual_layer_norm_rows(x_ref, h_ref, g_ref, b_ref, alpha, first_row, n_rows):
    gam = g_ref[...]
    bet = b_ref[...]
    group = LN_ROW_CHUNK * LN_UNROLL
    assert n_rows % group == 0
    for base in range(first_row, first_row + n_rows, group):
        chunks = [pl.ds(base + u * LN_ROW_CHUNK, LN_ROW_CHUNK) for u in range(LN_UNROLL)]
        rs = [alpha * x_ref[rows, :] + h_ref[rows, :] for rows in chunks]
        xcs = [r - jnp.mean(r, axis=-1, keepdims=True) for r in rs]
        rstds = [lax.rsqrt(jnp.mean(xc * xc, axis=-1, keepdims=True) + LN_EPS) for xc in xcs]
        for rows, xc, rstd in zip(chunks, xcs, rstds):
            h_ref[rows, :] = xc * rstd * gam + bet


def _cast_specs(w_stack, layer, n_blocks):
    _, rows, cols = w_stack.shape
    assert rows % (n_blocks * BF16_SUBLANES) == 0
    block = rows // n_blocks
    return (pl.BlockSpec((None, block, cols), lambda i, *_: (layer, i, 0)),
            pl.BlockSpec((block, cols), lambda i, *_: (i, 0)),
            jax.ShapeDtypeStruct((rows, cols), _BF16))


def _row_slabs(n_rows, n_slabs):
    assert n_rows % n_slabs == 0
    size = n_rows // n_slabs
    return [(s * size, size) for s in range(n_slabs)]


def _pool_in_kernel(x_ref, w_ref, cast_src_ref, p_ref, cast_dst_ref, carry_ref, *, blocks_per_seq):
    cast_dst_ref[...] = cast_src_ref[...].astype(cast_dst_ref.dtype)
    tm = x_ref.shape[0]
    gd = w_ref.shape[1] // len(POOL_WINDOWS)
    blk = pl.program_id(0) % blocks_per_seq

    @pl.when(blk == 0)
    def _():
        carry_ref[...] = jnp.zeros_like(carry_ref)

    xb = x_ref[...].astype(_BF16)
    pos = lax.broadcasted_iota(jnp.int32, (tm, 1), 0) + blk * tm
    for g, w in enumerate(POOL_WINDOWS):
        cols = slice(g * gd, (g + 1) * gd)
        u = jnp.dot(xb, w_ref[:, cols], preferred_element_type=_F32)
        t = jnp.concatenate([carry_ref[g], u], axis=0)
        carry_ref[g] = u[tm - MAX_POOL_WINDOW:, :]
        k = 1
        while k < w:
            t = t + pltpu.roll(t, k, 0)
            k *= 2
        inv_cnt = 1.0 / jnp.minimum(pos + 1, w).astype(_F32)
        p_ref[:, cols] = (t[MAX_POOL_WINDOW:, :] * inv_cnt - u).astype(p_ref.dtype)


def _pool_out_kernel(p_ref, x_ref, wg_ref, scale_ref, wo_ref, g_ref, b_ref, cast_src_ref, o_ref, cast_dst_ref, *,
                     alpha):
    cast_dst_ref[...] = cast_src_ref[...].astype(cast_dst_ref.dtype)
    n_groups, gd, _ = wg_ref.shape
    ys = []
    for g in range(n_groups):
        cols = slice(g * gd, (g + 1) * gd)
        y = jnp.dot(p_ref[:, cols], wg_ref[g], preferred_element_type=_F32) * scale_ref[:, cols]
        ys.append(y.astype(_BF16))
    yb = jnp.concatenate(ys, axis=1)
    for first, size in _row_slabs(x_ref.shape[0], EPILOGUE_SLABS):
        o_ref[first:first + size, :] = jnp.dot(yb[first:first + size, :], wo_ref[...], preferred_element_type=_F32)
        _residual_layer_norm_rows(x_ref, o_ref, g_ref, b_ref, alpha, first, size)


def _pool_layer(x, w_in, w_group, scale, w_out, ln_g, ln_b, cast_a, cast_b, *, seq, alpha):
    rows, d = x.shape
    n_groups, gd, _ = w_group.shape
    tm = min(512, seq)
    assert seq % tm == 0 and tm >= MAX_POOL_WINDOW and gd * n_groups == d
    row_block = pl.BlockSpec((tm, d), lambda i: (i, 0))
    a_in, a_out, a_shape = _cast_specs(*cast_a, rows // tm)
    b_in, b_out, b_shape = _cast_specs(*cast_b, rows // tm)
    p, a_bf16 = pl.pallas_call(
        functools.partial(_pool_in_kernel, blocks_per_seq=seq // tm),
        out_shape=(jax.ShapeDtypeStruct((rows, d), _BF16), a_shape),
        grid=(rows // tm,),
        in_specs=[row_block, _resident((d, d)), a_in],
        out_specs=(row_block, a_out),
        scratch_shapes=[pltpu.VMEM((n_groups, MAX_POOL_WINDOW, gd), _F32)],
        compiler_params=_params("arbitrary"),
        name="pool_in",
    )(x, w_in.astype(_BF16), cast_a[0])
    h, b_bf16 = pl.pallas_call(
        functools.partial(_pool_out_kernel, alpha=alpha),
        out_shape=(jax.ShapeDtypeStruct((rows, d), _F32), b_shape),
        grid=(rows // tm,),
        in_specs=[row_block, row_block, _resident((n_groups, gd, gd)), _resident((1, d)),
                  _resident((d, d)), _resident((1, d)), _resident((1, d)), b_in],
        out_specs=(row_block, b_out),
        compiler_params=_params("parallel"),
        name="pool_out",
    )(p, x, w_group.astype(_BF16), scale.reshape(1, d), w_out.astype(_BF16),
      ln_g.reshape(1, d), ln_b.reshape(1, d), cast_b[0])
    return h, a_bf16, b_bf16


def _mlp_kernel(x_ref, wu_ref, wd_ref, g_ref, b_ref, o_ref, xb_ref, *, alpha):
    j = pl.program_id(1)

    @pl.when(j == 0)
    def _():
        xb_ref[...] = x_ref[...].astype(_BF16)
        o_ref[...] = jnp.zeros_like(o_ref)

    n_sub = wu_ref.shape[1] // MLP_SUB_CHUNK
    last = pl.num_programs(1) - 1

    def feed_forward(rows, c):
        cols = slice(c * MLP_SUB_CHUNK, (c + 1) * MLP_SUB_CHUNK)
        h = jnp.maximum(jnp.dot(xb_ref[rows, :], wu_ref[:, cols], preferred_element_type=_F32), 0.0)
        o_ref[rows, :] += jnp.dot((h * h).astype(_BF16), wd_ref[cols, :], preferred_element_type=_F32)

    @pl.when(j < last)
    def _():
        for c in range(n_sub):
            feed_forward(slice(None), c)

    @pl.when(j == last)
    def _():
        for c in range(n_sub - 1):
            feed_forward(slice(None), c)
        for first, size in _row_slabs(x_ref.shape[0], EPILOGUE_SLABS):
            feed_forward(slice(first, first + size), n_sub - 1)
            _residual_layer_norm_rows(x_ref, o_ref, g_ref, b_ref, alpha, first, size)


def _mlp_layer(x, w_up, w_down, ln_g, ln_b, *, alpha):
    rows, d = x.shape
    d_ff = w_up.shape[1]
    tm = min(1024, rows)
    tf = min(MLP_FF_BLOCK, d_ff)
    assert rows % tm == 0 and d_ff % tf == 0 and tf % MLP_SUB_CHUNK == 0
    return pl.pallas_call(
        functools.partial(_mlp_kernel, alpha=alpha),
        out_shape=jax.ShapeDtypeStruct((rows, d), _F32),
        grid=(rows // tm, d_ff // tf),
        in_specs=[pl.BlockSpec((tm, d), lambda i, j: (i, 0)),
                  pl.BlockSpec((d, tf), lambda i, j: (0, j)),
                  pl.BlockSpec((tf, d), lambda i, j: (j, 0)),
                  _resident((1, d)), _resident((1, d))],
        out_specs=pl.BlockSpec((tm, d), lambda i, j: (i, 0)),
        scratch_shapes=[pltpu.VMEM((tm, d), _BF16)],
        compiler_params=_params("parallel", "arbitrary"),
        name="sqrelu_mlp",
    )(x, w_up, w_down, ln_g.reshape(1, d), ln_b.reshape(1, d))


def _qkv_kernel(*refs, dil, has_cast):
    if has_cast:
        x_ref, w_ref, cast_src_ref, o_ref, cast_dst_ref, xb_ref, *scratch = refs
    else:
        x_ref, w_ref, o_ref, xb_ref, *scratch = refs

    @pl.when(pl.program_id(1) == 0)
    def _():
        xb_ref[...] = x_ref[...].astype(_BF16)
        if has_cast:
            cast_dst_ref[...] = cast_src_ref[...].astype(cast_dst_ref.dtype)

    res = jnp.dot(xb_ref[...], w_ref[...], preferred_element_type=_F32)
    if dil == 1:
        o_ref[0] = res.astype(o_ref.dtype)
        return
    res_ref = scratch[0]
    tm = res_ref.shape[1]
    n = tm // dil
    col_tiles = [slice(c * LANES, (c + 1) * LANES) for c in range(res_ref.shape[0])]
    for c, cols in enumerate(col_tiles):
        res_ref[c] = res[:, cols]
    if dil <= QKV_MAX_STRIDE:
        for r in range(dil):
            for c, cols in enumerate(col_tiles):
                o_ref[r, :, cols] = res_ref[c, pl.ds(r, n, stride=dil), :].astype(o_ref.dtype)
        return
    tmp_ref = scratch[1]
    s1 = QKV_MAX_STRIDE
    s2 = dil // s1
    for b in range(s1):
        for c in range(len(col_tiles)):
            tmp_ref[c, b * (tm // s1):(b + 1) * (tm // s1), :] = res_ref[c, pl.ds(b, tm // s1, stride=s1), :]
    for r in range(dil):
        for c, cols in enumerate(col_tiles):
            src = pl.ds((r % s1) * (tm // s1) + r // s1, n, stride=s2)
            o_ref[r, :, cols] = tmp_ref[c, src, :].astype(o_ref.dtype)


def _qkv_group(x, w_qkv, cast, *, batch, seq, group, n_groups, gw):
    rows, d = x.shape
    _, dil = ATTN_PATTERNS[group]
    tm = min(1024, seq)
    assert seq % tm == 0 and tm % (dil * BF16_SUBLANES) == 0
    assert dil <= QKV_MAX_STRIDE or dil % QKV_MAX_STRIDE == 0
    blocks_per_seq = seq // tm
    scratch = [pltpu.VMEM((tm, d), _BF16)]
    scratch += [pltpu.VMEM((gw // LANES, tm, LANES), _F32)] * ((dil > 1) + (dil > QKV_MAX_STRIDE))
    in_specs = [pl.BlockSpec((tm, d), lambda i, j: (i, 0)),
                pl.BlockSpec((d, gw), lambda i, j: (0, j * n_groups + group))]
    out_specs = [pl.BlockSpec((None, None, dil, tm // dil, gw),
                              lambda i, j: (i // blocks_per_seq, j, 0, i % blocks_per_seq, 0))]
    out_shape = [jax.ShapeDtypeStruct((batch, 3, dil, seq // dil, gw), _BF16)]
    operands = [x, w_qkv]
    if cast is not None:
        c_in, c_out, c_shape = _cast_specs(*cast, rows // tm)
        in_specs.append(c_in)
        out_specs.append(c_out)
        out_shape.append(c_shape)
        operands.append(cast[0])
    outs = pl.pallas_call(
        functools.partial(_qkv_kernel, dil=dil, has_cast=cast is not None),
        out_shape=tuple(out_shape),
        grid=(rows // tm, 3),
        in_specs=in_specs,
        out_specs=tuple(out_specs),
        scratch_shapes=scratch,
        compiler_params=_params("parallel", "arbitrary"),
        name=f"qkv_proj_g{group}",
    )(*operands)
    return outs if cast is not None else (outs[0], None)


def _attn_kernel(slope_ref, q_ref, k_ref, v_ref, o_ref, lse_ref, osc_ref, bias_ref, *, dil, n_back):
    sub_len = q_ref.shape[1]
    n_heads = q_ref.shape[2] // HEAD_DIM
    nb = sub_len // ATTN_BLOCK
    nq = ATTN_BLOCK
    nk = bias_ref.shape[-1]
    lane = lax.broadcasted_iota(jnp.int32, (nq, LANES), 1)
    qk_scale = LOG2_E / math.sqrt(HEAD_DIM)
    cols = [slice(h * HEAD_DIM, (h + 1) * HEAD_DIM) for h in range(n_heads)]

    @pl.when(pl.program_id(0) == 0)
    def _():
        q_minus_k = lax.broadcasted_iota(jnp.int32, (nq, nk), 0) - lax.broadcasted_iota(jnp.int32, (nq, nk), 1)
        for i in range(bias_ref.shape[0]):
            dist = q_minus_k + i * nq
            valid = (dist >= 0) & (dist <= n_back)
            neg_dist = jnp.where(valid, -(dist * dil).astype(_F32), -MASK_DISTANCE)
            for h in range(n_heads):
                bias_ref[i, h] = (slope_ref[h] * LOG2_E) * neg_dist

    def block(p, carry):
        r = p // nb
        q0 = pl.multiple_of((p % nb) * nq, nq)
        k0 = pl.multiple_of(jnp.maximum(q0 - nq, 0), nq) if nb > 1 else 0
        back = (q0 - k0) // nq
        tok = pl.ds(q0 * dil + r, nq, stride=dil) if dil > 1 else pl.ds(q0, nq)
        ss = [lax.dot_general(q_ref[r, pl.ds(q0, nq), c], k_ref[r, pl.ds(k0, nk), c], (((1,), (1,)), ((), ())),
                              preferred_element_type=_F32) * qk_scale + bias_ref[back, h]
              for h, c in enumerate(cols)]
        ms = [jnp.max(s, axis=-1, keepdims=True) for s in ss]
        es = [jnp.exp2(s - m) for s, m in zip(ss, ms)]
        dens = [jnp.sum(e, axis=-1, keepdims=True) for e in es]
        os = [jnp.dot(e.astype(_BF16), v_ref[r, pl.ds(k0, nk), c], preferred_element_type=_F32)
              for e, c in zip(es, cols)]
        m_all = jnp.zeros((nq, LANES), _F32)
        den_all = jnp.ones((nq, LANES), _F32)
        for h in range(n_heads):
            osc_ref.at[h][tok, :] = os[h] * (1.0 / dens[h])
            m_all = jnp.where(lane == h, ms[h], m_all)
            den_all = jnp.where(lane == h, dens[h], den_all)
        lse_ref[tok, :] = (m_all + jnp.log2(den_all)) * LN_2
        return carry

    lax.fori_loop(0, dil * nb, block, 0)
    for h in range(n_heads):
        o_ref[:, cols[h]] = osc_ref[h].astype(o_ref.dtype)


def _attn_group(qkv, slopes, *, batch, seq, group, gw):
    window, dil = ATTN_PATTERNS[group]
    sub_len = seq // dil
    assert window // dil == ATTN_BLOCK and sub_len % ATTN_BLOCK == 0 and gw // HEAD_DIM <= LANES
    n_heads = gw // HEAD_DIM
    windows = 2 if sub_len > ATTN_BLOCK else 1

    def part(j):
        return pl.BlockSpec((None, None, dil, sub_len, gw), lambda b: (b, j, 0, 0, 0))

    return pl.pallas_call(
        functools.partial(_attn_kernel, dil=dil, n_back=window // dil),
        out_shape=(jax.ShapeDtypeStruct((batch * seq, gw), _BF16),
                   jax.ShapeDtypeStruct((batch * seq, LANES), _F32)),
        grid=(batch,),
        in_specs=[pl.BlockSpec(memory_space=pltpu.SMEM), part(0), part(1), part(2)],
        out_specs=(pl.BlockSpec((seq, gw), lambda b: (b, 0)), pl.BlockSpec((seq, LANES), lambda b: (b, 0))),
        scratch_shapes=[pltpu.VMEM((n_heads, seq, HEAD_DIM), _F32),
                        pltpu.VMEM((windows, n_heads, ATTN_BLOCK, windows * ATTN_BLOCK), _F32)],
        compiler_params=_params("arbitrary"),
        name=f"dilated_attn_g{group}",
    )(slopes, qkv, qkv, qkv)


def _attn_out_kernel(*refs, alpha, n_groups):
    o_refs = refs[:n_groups]
    l_refs = refs[n_groups:2 * n_groups]
    x_ref, w_ref, g_ref, b_ref, out_ref = refs[2 * n_groups:]
    gw = w_ref.shape[0]
    expand = (lax.broadcasted_iota(jnp.int32, (2 * LANES, gw), 1) // HEAD_DIM
              == lax.broadcasted_iota(jnp.int32, (2 * LANES, gw), 0) % LANES).astype(_BF16)

    lses = [l[...] for l in l_refs]
    m = functools.reduce(jnp.maximum, lses)
    es = [jnp.exp(l - m) for l in lses]
    inv = 1.0 / functools.reduce(jnp.add, es)
    acc = None
    for e, o_ref in zip(es, o_refs):
        w = e * inv
        w_hi = w.astype(_BF16)
        w_lo = (w - w_hi.astype(_F32)).astype(_BF16)
        wide = jnp.dot(jnp.concatenate([w_hi, w_lo], axis=1), expand, preferred_element_type=_F32)
        term = wide * o_ref[...].astype(_F32)
        acc = term if acc is None else acc + term
    merged = acc.astype(_BF16)
    for first, size in _row_slabs(x_ref.shape[0], EPILOGUE_SLABS):
        out_ref[first:first + size, :] = jnp.dot(merged[first:first + size, :], w_ref[...],
                                                 preferred_element_type=_F32)
        _residual_layer_norm_rows(x_ref, out_ref, g_ref, b_ref, alpha, first, size)


def _alibi_slopes(n_groups, n_slots):
    n = n_groups * n_slots
    return jnp.exp2(-8.0 * jnp.arange(1, n + 1, dtype=_F32) / n).reshape(n_groups, n_slots)


def _attn_layer(x, w_qkv, w_out, ln_g, ln_b, cast_a, cast_b, *, batch, seq, alpha):
    rows, d = x.shape
    n_groups = len(ATTN_PATTERNS)
    gw = w_out.shape[0]
    assert w_qkv.shape[1] == 3 * n_groups * gw and n_groups >= 2
    w_qkv = w_qkv.astype(_BF16)
    slopes = _alibi_slopes(n_groups, gw // HEAD_DIM)
    outs, lses, casts = [], [], []
    for g, cast in zip(range(n_groups), (cast_a, cast_b) + (None,) * (n_groups - 2)):
        qkv, cast_bf16 = _qkv_group(x, w_qkv, cast, batch=batch, seq=seq, group=g, n_groups=n_groups, gw=gw)
        casts.append(cast_bf16)
        o, lse = _attn_group(qkv, slopes[g], batch=batch, seq=seq, group=g, gw=gw)
        outs.append(o)
        lses.append(lse)
    tm = min(512, rows)
    head_block = pl.BlockSpec((tm, gw), lambda i: (i, 0))
    lse_block = pl.BlockSpec((tm, LANES), lambda i: (i, 0))
    row_block = pl.BlockSpec((tm, d), lambda i: (i, 0))
    h = pl.pallas_call(
        functools.partial(_attn_out_kernel, alpha=alpha, n_groups=n_groups),
        out_shape=jax.ShapeDtypeStruct((rows, d), _F32),
        grid=(rows // tm,),
        in_specs=[head_block] * n_groups + [lse_block] * n_groups
        + [row_block, _resident((gw, d)), _resident((1, d)), _resident((1, d))],
        out_specs=row_block,
        compiler_params=_params("parallel"),
        name="attn_merge_out",
    )(*outs, *lses, x, w_out.astype(_BF16), ln_g.reshape(1, d), ln_b.reshape(1, d))
    return h, casts[0], casts[1]


def kernel(x, pool_w_in, pool_w_group, pool_scale, pool_w_out, attn_w_qkv, attn_w_out, mlp_w_up, mlp_w_down, ln_mix_g, ln_mix_b, ln_mlp_g, ln_mlp_b):
    batch, seq, d = x.shape
    depth = mlp_w_up.shape[0]
    alpha = (2 * depth) ** 0.25
    h = x.reshape(batch * seq, d)
    for i in range(depth):
        j = i // 2
        casts = ((mlp_w_up, i), (mlp_w_down, i))
        if i % 2 == 0:
            h, w_up, w_down = _pool_layer(h, pool_w_in[j], pool_w_group[j], pool_scale[j], pool_w_out[j],
                                          ln_mix_g[i], ln_mix_b[i], *casts, seq=seq, alpha=alpha)
        else:
            h, w_up, w_down = _attn_layer(h, attn_w_qkv[j], attn_w_out[j], ln_mix_g[i], ln_mix_b[i], *casts,
                                          batch=batch, seq=seq, alpha=alpha)
        h = _mlp_layer(h, w_up, w_down, ln_mlp_g[i], ln_mlp_b[i], alpha=alpha)
    return h.reshape(batch, seq, d)
```

```python
import functools
import math

import jax
import jax.numpy as jnp
from jax import lax
from jax.experimental import pallas as pl
from jax.experimental.pallas import tpu as pltpu

POOL_WINDOWS = (2, 4, 8, 16)
MAX_POOL_WINDOW = max(POOL_WINDOWS)
ATTN_PATTERNS = ((128, 1), (512, 4), (2048, 16))
HEAD_DIM = 128
ATTN_BLOCK = 128
LN_EPS = 1e-5
MASK_DISTANCE = 1e30
LOG2_E = math.log2(math.e)
LN_2 = math.log(2.0)

LANES = 128
BF16_SUBLANES = 16
V7X_VMEM_LIMIT_BYTES = 60 * 1024 * 1024
LN_ROW_CHUNK = 32
LN_UNROLL = 4
EPILOGUE_SLABS = 4
MLP_FF_BLOCK = 1024
MLP_SUB_CHUNK = 512
QKV_MAX_STRIDE = 4

_F32 = jnp.float32
_BF16 = jnp.bfloat16


def _params(*semantics):
    return pltpu.CompilerParams(dimension_semantics=semantics, vmem_limit_bytes=V7X_VMEM_LIMIT_BYTES)


def _resident(shape):
    return pl.BlockSpec(shape, lambda *_: (0,) * len(shape), pipeline_mode=pl.Buffered(1))


def _residual_layer_norm_rows(x_ref, h_ref, g_ref, b_ref, alpha, first_row, n_rows):
    gam = g_ref[...]
    bet = b_ref[...]
    group = LN_ROW_CHUNK * LN_UNROLL
    assert n_rows % group == 0
    for offset in range(0, n_rows, group):
        chunks = [pl.ds(first_row + offset + u * LN_ROW_CHUNK, LN_ROW_CHUNK) for u in range(LN_UNROLL)]
        rs = [alpha * x_ref[rows, :] + h_ref[rows, :] for rows in chunks]
        xcs = [r - jnp.mean(r, axis=-1, keepdims=True) for r in rs]
        rstds = [lax.rsqrt(jnp.mean(xc * xc, axis=-1, keepdims=True) + LN_EPS) for xc in xcs]
        for rows, xc, rstd in zip(chunks, xcs, rstds):
            h_ref[rows, :] = xc * rstd * gam + bet


def _cast_specs(w_stack, layer, n_blocks):
    _, rows, cols = w_stack.shape
    assert rows % (n_blocks * BF16_SUBLANES) == 0
    block = rows // n_blocks
    return (pl.BlockSpec((None, block, cols), lambda i, *_: (layer, i, 0)),
            pl.BlockSpec((block, cols), lambda i, *_: (i, 0)),
            jax.ShapeDtypeStruct((rows, cols), _BF16))


def _row_slabs(n_rows, n_slabs):
    assert n_rows % n_slabs == 0
    size = n_rows // n_slabs
    return [(s * size, size) for s in range(n_slabs)]


def _cast_blocks(src_refs, dst_refs):
    for src_ref, dst_ref in zip(src_refs, dst_refs):
        dst_ref[...] = src_ref[...].astype(dst_ref.dtype)


def _pool_in_kernel(*refs, blocks_per_seq, n_casts):
    x_ref, w_ref = refs[:2]
    p_ref = refs[2 + n_casts]
    carry_ref = refs[-1]
    _cast_blocks(refs[2:2 + n_casts], refs[3 + n_casts:-1])
    tm = x_ref.shape[0]
    gd = w_ref.shape[1] // len(POOL_WINDOWS)
    blk = pl.program_id(0) % blocks_per_seq

    @pl.when(blk == 0)
    def _():
        carry_ref[...] = jnp.zeros_like(carry_ref)

    xb = x_ref[...].astype(_BF16)
    pos = lax.broadcasted_iota(jnp.int32, (tm, 1), 0) + blk * tm
    for g, w in reversed(list(enumerate(POOL_WINDOWS))):
        cols = slice(g * gd, (g + 1) * gd)
        u = jnp.dot(xb, w_ref[:, cols], preferred_element_type=_F32)
        t = jnp.concatenate([carry_ref[g], u], axis=0)
        carry_ref[g] = u[tm - MAX_POOL_WINDOW:, :]
        k = 1
        while k < w:
            t = t + pltpu.roll(t, k, 0)
            k *= 2
        inv_cnt = 1.0 / jnp.minimum(pos + 1, w).astype(_F32)
        p_ref[:, cols] = (t[MAX_POOL_WINDOW:, :] * inv_cnt - u).astype(p_ref.dtype)


def _pool_out_kernel(*refs, alpha, n_casts):
    p_ref, x_ref, wg_ref, scale_ref, wo_ref, g_ref, b_ref = refs[:7]
    o_ref = refs[7 + n_casts]
    _cast_blocks(refs[7:7 + n_casts], refs[8 + n_casts:])
    n_groups, gd, _ = wg_ref.shape
    ys = []
    for g in range(n_groups):
        cols = slice(g * gd, (g + 1) * gd)
        y = jnp.dot(p_ref[:, cols], wg_ref[g], preferred_element_type=_F32) * scale_ref[:, cols]
        ys.append(y.astype(_BF16))
    yb = jnp.concatenate(ys, axis=1)
    for first, size in _row_slabs(x_ref.shape[0], EPILOGUE_SLABS):
        o_ref[first:first + size, :] = jnp.dot(yb[first:first + size, :], wo_ref[...], preferred_element_type=_F32)
        _residual_layer_norm_rows(x_ref, o_ref, g_ref, b_ref, alpha, first, size)


def _pool_layer(x, w_in, w_group, scale, w_out, ln_g, ln_b, casts_a, casts_b, *, seq, alpha):
    rows, d = x.shape
    n_groups, gd, _ = w_group.shape
    tm = min(512, seq)
    assert seq % tm == 0 and tm >= MAX_POOL_WINDOW and gd * n_groups == d
    row_block = pl.BlockSpec((tm, d), lambda i: (i, 0))
    specs_a = [_cast_specs(*c, rows // tm) for c in casts_a]
    specs_b = [_cast_specs(*c, rows // tm) for c in casts_b]
    p, *a_bf16 = pl.pallas_call(
        functools.partial(_pool_in_kernel, blocks_per_seq=seq // tm, n_casts=len(casts_a)),
        out_shape=(jax.ShapeDtypeStruct((rows, d), _BF16), *[s[2] for s in specs_a]),
        grid=(rows // tm,),
        in_specs=[row_block, _resident((d, d)), *[s[0] for s in specs_a]],
        out_specs=(row_block, *[s[1] for s in specs_a]),
        scratch_shapes=[pltpu.VMEM((n_groups, MAX_POOL_WINDOW, gd), _F32)],
        compiler_params=_params("arbitrary"),
        name="pool_in",
    )(x, w_in.astype(_BF16), *[c[0] for c in casts_a])
    h, *b_bf16 = pl.pallas_call(
        functools.partial(_pool_out_kernel, alpha=alpha, n_casts=len(casts_b)),
        out_shape=(jax.ShapeDtypeStruct((rows, d), _F32), *[s[2] for s in specs_b]),
        grid=(rows // tm,),
        in_specs=[row_block, row_block, _resident((n_groups, gd, gd)), _resident((1, d)),
                  _resident((d, d)), _resident((1, d)), _resident((1, d)), *[s[0] for s in specs_b]],
        out_specs=(row_block, *[s[1] for s in specs_b]),
        compiler_params=_params("parallel"),
        name="pool_out",
    )(p, x, w_group.astype(_BF16), scale.reshape(1, d), w_out.astype(_BF16),
      ln_g.reshape(1, d), ln_b.reshape(1, d), *[c[0] for c in casts_b])
    return h, a_bf16, b_bf16


def _mlp_kernel(x_ref, wu_ref, wd_ref, g_ref, b_ref, o_ref, xb_ref, *, alpha):
    j = pl.program_id(1)

    @pl.when(j == 0)
    def _():
        xb_ref[...] = x_ref[...].astype(_BF16)
        o_ref[...] = jnp.zeros_like(o_ref)

    for c in range(wu_ref.shape[1] // MLP_SUB_CHUNK):
        cols = slice(c * MLP_SUB_CHUNK, (c + 1) * MLP_SUB_CHUNK)
        h = jnp.maximum(jnp.dot(xb_ref[...], wu_ref[:, cols], preferred_element_type=_F32), 0.0)
        o_ref[...] += jnp.dot((h * h).astype(_BF16), wd_ref[cols, :], preferred_element_type=_F32)

    @pl.when(j == pl.num_programs(1) - 1)
    def _():
        group = LN_ROW_CHUNK * LN_UNROLL

        def step(c, carry):
            _residual_layer_norm_rows(x_ref, o_ref, g_ref, b_ref, alpha, pl.multiple_of(c * group, group), group)
            return carry

        lax.fori_loop(0, x_ref.shape[0] // group, step, 0)


def _mlp_layer(x, w_up, w_down, ln_g, ln_b, *, alpha):
    rows, d = x.shape
    d_ff = w_up.shape[1]
    tm = min(1024, rows)
    tf = min(MLP_FF_BLOCK, d_ff)
    assert rows % tm == 0 and d_ff % tf == 0 and tf % MLP_SUB_CHUNK == 0
    return pl.pallas_call(
        functools.partial(_mlp_kernel, alpha=alpha),
        out_shape=jax.ShapeDtypeStruct((rows, d), _F32),
        grid=(rows // tm, d_ff // tf),
        in_specs=[pl.BlockSpec((tm, d), lambda i, j: (i, 0)),
                  pl.BlockSpec((d, tf), lambda i, j: (0, j)),
                  pl.BlockSpec((tf, d), lambda i, j: (j, 0)),
                  _resident((1, d)), _resident((1, d))],
        out_specs=pl.BlockSpec((tm, d), lambda i, j: (i, 0)),
        scratch_shapes=[pltpu.VMEM((tm, d), _BF16)],
        compiler_params=_params("parallel", "arbitrary"),
        name="sqrelu_mlp",
    )(x, w_up, w_down, ln_g.reshape(1, d), ln_b.reshape(1, d))


def _qkv_kernel(*refs, dil, emit_bf16):
    if emit_bf16:
        x_ref, w_ref, o_ref, xb_ref, *scratch = refs

        @pl.when(pl.program_id(1) == 0)
        def _():
            xb_ref[...] = x_ref[...].astype(xb_ref.dtype)
    else:
        xb_ref, w_ref, o_ref, *scratch = refs

    res = jnp.dot(xb_ref[...], w_ref[...], preferred_element_type=_F32)
    if dil == 1:
        o_ref[0] = res.astype(o_ref.dtype)
        return
    res_ref = scratch[0]
    tm = res_ref.shape[1]
    n = tm // dil
    col_tiles = [slice(c * LANES, (c + 1) * LANES) for c in range(res_ref.shape[0])]
    for c, cols in enumerate(col_tiles):
        res_ref[c] = res[:, cols]
    if dil <= QKV_MAX_STRIDE:
        for r in range(dil):
            for c, cols in enumerate(col_tiles):
                o_ref[r, :, cols] = res_ref[c, pl.ds(r, n, stride=dil), :].astype(o_ref.dtype)
        return
    tmp_ref = scratch[1]
    s1 = QKV_MAX_STRIDE
    s2 = dil // s1
    for b in range(s1):
        for c in range(len(col_tiles)):
            tmp_ref[c, b * (tm // s1):(b + 1) * (tm // s1), :] = res_ref[c, pl.ds(b, tm // s1, stride=s1), :]
    for r in range(dil):
        for c, cols in enumerate(col_tiles):
            src = pl.ds((r % s1) * (tm // s1) + r // s1, n, stride=s2)
            o_ref[r, :, cols] = tmp_ref[c, src, :].astype(o_ref.dtype)


def _qkv_group(x, w_qkv, *, batch, seq, group, n_groups, gw):
    rows, d = x.shape
    _, dil = ATTN_PATTERNS[group]
    emit_bf16 = x.dtype != _BF16
    tm = min(1024, seq)
    assert seq % tm == 0 and tm % (dil * BF16_SUBLANES) == 0
    assert dil <= QKV_MAX_STRIDE or dil % QKV_MAX_STRIDE == 0
    blocks_per_seq = seq // tm
    x_block = pl.BlockSpec((tm, d), lambda i, j: (i, 0))
    qkv_shape = jax.ShapeDtypeStruct((batch, 3, dil, seq // dil, gw), _BF16)
    qkv_block = pl.BlockSpec((None, None, dil, tm // dil, gw),
                             lambda i, j: (i // blocks_per_seq, j, 0, i % blocks_per_seq, 0))
    outs = pl.pallas_call(
        functools.partial(_qkv_kernel, dil=dil, emit_bf16=emit_bf16),
        out_shape=(qkv_shape, jax.ShapeDtypeStruct((rows, d), _BF16)) if emit_bf16 else qkv_shape,
        grid=(rows // tm, 3),
        in_specs=[x_block, pl.BlockSpec((d, gw), lambda i, j: (0, j * n_groups + group))],
        out_specs=(qkv_block, x_block) if emit_bf16 else qkv_block,
        scratch_shapes=[pltpu.VMEM((gw // LANES, tm, LANES), _F32)] * ((dil > 1) + (dil > QKV_MAX_STRIDE)),
        compiler_params=_params("parallel", "arbitrary"),
        name=f"qkv_proj_g{group}",
    )(x, w_qkv)
    return outs if emit_bf16 else (outs, x)


def _attn_kernel(slope_ref, q_ref, k_ref, v_ref, o_ref, lse_ref, osc_ref, bias_ref, *, dil, n_back):
    sub_len = q_ref.shape[1]
    n_heads = q_ref.shape[2] // HEAD_DIM
    nb = sub_len // ATTN_BLOCK
    nq = ATTN_BLOCK
    nk = bias_ref.shape[-1]
    lane = lax.broadcasted_iota(jnp.int32, (nq, LANES), 1)
    qk_scale = LOG2_E / math.sqrt(HEAD_DIM)
    cols = [slice(h * HEAD_DIM, (h + 1) * HEAD_DIM) for h in range(n_heads)]
    ones = jnp.ones((nk, HEAD_DIM), _BF16)

    @pl.when(pl.program_id(0) == 0)
    def _():
        q_minus_k = lax.broadcasted_iota(jnp.int32, (nq, nk), 0) - lax.broadcasted_iota(jnp.int32, (nq, nk), 1)
        for i in range(bias_ref.shape[0]):
            dist = q_minus_k + i * nq
            valid = (dist >= 0) & (dist <= n_back)
            neg_dist = jnp.where(valid, -(dist * dil).astype(_F32), -MASK_DISTANCE)
            for h in range(n_heads):
                bias_ref[i, h] = (slope_ref[h] * LOG2_E) * neg_dist

    def block(p, carry):
        r = p // nb
        q0 = pl.multiple_of((p % nb) * nq, nq)
        k0 = pl.multiple_of(jnp.maximum(q0 - nq, 0), nq) if nb > 1 else 0
        back = (q0 - k0) // nq
        tok = pl.ds(q0 * dil + r, nq, stride=dil) if dil > 1 else pl.ds(q0, nq)
        ss = [lax.dot_general(q_ref[r, pl.ds(q0, nq), c], k_ref[r, pl.ds(k0, nk), c], (((1,), (1,)), ((), ())),
                              preferred_element_type=_F32) * qk_scale + bias_ref[back, h]
              for h, c in enumerate(cols)]
        ms = [jnp.max(s, axis=-1, keepdims=True) for s in ss]
        es = [jnp.exp2(s - m).astype(_BF16) for s, m in zip(ss, ms)]
        os = [jnp.dot(e, jnp.concatenate([v_ref[r, pl.ds(k0, nk), c], ones], axis=1), preferred_element_type=_F32)
              for e, c in zip(es, cols)]
        m_all = jnp.zeros((nq, LANES), _F32)
        den_all = jnp.ones((nq, LANES), _F32)
        for h in range(n_heads):
            den = os[h][:, HEAD_DIM:]
            osc_ref.at[h][tok, :] = os[h][:, :HEAD_DIM] * (1.0 / den)
            m_all = jnp.where(lane == h, ms[h], m_all)
            den_all = jnp.where(lane == h, den, den_all)
        lse_ref[tok, :] = (m_all + jnp.log2(den_all)) * LN_2
        return carry

    lax.fori_loop(0, dil * nb, block, 0)
    for h in range(n_heads):
        o_ref[:, cols[h]] = osc_ref[h].astype(o_ref.dtype)


def _attn_group(qkv, slopes, *, batch, seq, group, gw):
    window, dil = ATTN_PATTERNS[group]
    sub_len = seq // dil
    assert window // dil == ATTN_BLOCK and sub_len % ATTN_BLOCK == 0 and gw // HEAD_DIM <= LANES == HEAD_DIM
    n_heads = gw // HEAD_DIM
    windows = 2 if sub_len > ATTN_BLOCK else 1

    def part(j):
        return pl.BlockSpec((None, None, dil, sub_len, gw), lambda b: (b, j, 0, 0, 0))

    return pl.pallas_call(
        functools.partial(_attn_kernel, dil=dil, n_back=window // dil),
        out_shape=(jax.ShapeDtypeStruct((batch * seq, gw), _BF16),
                   jax.ShapeDtypeStruct((batch * seq, LANES), _F32)),
        grid=(batch,),
        in_specs=[pl.BlockSpec(memory_space=pltpu.SMEM), part(0), part(1), part(2)],
        out_specs=(pl.BlockSpec((seq, gw), lambda b: (b, 0)), pl.BlockSpec((seq, LANES), lambda b: (b, 0))),
        scratch_shapes=[pltpu.VMEM((n_heads, seq, HEAD_DIM), _F32),
                        pltpu.VMEM((windows, n_heads, ATTN_BLOCK, windows * ATTN_BLOCK), _F32)],
        compiler_params=_params("arbitrary"),
        name=f"dilated_attn_g{group}",
    )(slopes, qkv, qkv, qkv)


def _attn_out_kernel(*refs, alpha, n_groups):
    o_refs = refs[:n_groups]
    l_refs = refs[n_groups:2 * n_groups]
    x_ref, w_ref, g_ref, b_ref, out_ref = refs[2 * n_groups:]
    gw = w_ref.shape[0]
    expand = (lax.broadcasted_iota(jnp.int32, (2 * LANES, gw), 1) // HEAD_DIM
              == lax.broadcasted_iota(jnp.int32, (2 * LANES, gw), 0) % LANES).astype(_BF16)

    lses = [l[...] for l in l_refs]
    m = functools.reduce(jnp.maximum, lses)
    es = [jnp.exp(l - m) for l in lses]
    inv = 1.0 / functools.reduce(jnp.add, es)
    acc = None
    for e, o_ref in zip(es, o_refs):
        w = e * inv
        w_hi = w.astype(_BF16)
        w_lo = (w - w_hi.astype(_F32)).astype(_BF16)
        wide = jnp.dot(jnp.concatenate([w_hi, w_lo], axis=1), expand, preferred_element_type=_F32)
        term = wide * o_ref[...].astype(_F32)
        acc = term if acc is None else acc + term
    merged = acc.astype(_BF16)
    for first, size in _row_slabs(x_ref.shape[0], EPILOGUE_SLABS):
        out_ref[first:first + size, :] = jnp.dot(merged[first:first + size, :], w_ref[...],
                                                 preferred_element_type=_F32)
        _residual_layer_norm_rows(x_ref, out_ref, g_ref, b_ref, alpha, first, size)


def _alibi_slopes(n_groups, n_slots):
    n = n_groups * n_slots
    return jnp.exp2(-8.0 * jnp.arange(1, n + 1, dtype=_F32) / n).reshape(n_groups, n_slots)


def _attn_layer(x, w_qkv, w_out, ln_g, ln_b, *, batch, seq, alpha):
    rows, d = x.shape
    n_groups = len(ATTN_PATTERNS)
    gw = w_out.shape[0]
    assert w_qkv.shape[1] == 3 * n_groups * gw
    w_qkv = w_qkv.astype(_BF16)
    slopes = _alibi_slopes(n_groups, gw // HEAD_DIM)
    outs, lses = [], []
    x_mxu = x
    for g in range(n_groups):
        qkv, x_mxu = _qkv_group(x_mxu, w_qkv, batch=batch, seq=seq, group=g, n_groups=n_groups, gw=gw)
        o, lse = _attn_group(qkv, slopes[g], batch=batch, seq=seq, group=g, gw=gw)
        outs.append(o)
        lses.append(lse)
    tm = min(512, rows)
    head_block = pl.BlockSpec((tm, gw), lambda i: (i, 0))
    lse_block = pl.BlockSpec((tm, LANES), lambda i: (i, 0))
    row_block = pl.BlockSpec((tm, d), lambda i: (i, 0))
    return pl.pallas_call(
        functools.partial(_attn_out_kernel, alpha=alpha, n_groups=n_groups),
        out_shape=jax.ShapeDtypeStruct((rows, d), _F32),
        grid=(rows // tm,),
        in_specs=[head_block] * n_groups + [lse_block] * n_groups
        + [row_block, _resident((gw, d)), _resident((1, d)), _resident((1, d))],
        out_specs=row_block,
        compiler_params=_params("parallel"),
        name="attn_merge_out",
    )(*outs, *lses, x, w_out.astype(_BF16), ln_g.reshape(1, d), ln_b.reshape(1, d))


def kernel(x, pool_w_in, pool_w_group, pool_scale, pool_w_out, attn_w_qkv, attn_w_out, mlp_w_up, mlp_w_down, ln_mix_g, ln_mix_b, ln_mlp_g, ln_mlp_b):
    batch, seq, d = x.shape
    depth = mlp_w_up.shape[0]
    alpha = (2 * depth) ** 0.25
    h = x.reshape(batch * seq, d)
    mlp_weights = {}
    for i in range(depth):
        j = i // 2
        if i % 2 == 0:
            layers = [l for l in (i, i + 1) if l < depth]
            h, ups, downs = _pool_layer(h, pool_w_in[j], pool_w_group[j], pool_scale[j], pool_w_out[j],
                                        ln_mix_g[i], ln_mix_b[i], [(mlp_w_up, l) for l in layers],
                                        [(mlp_w_down, l) for l in layers], seq=seq, alpha=alpha)
            mlp_weights.update(zip(layers, zip(ups, downs)))
        else:
            h = _attn_layer(h, attn_w_qkv[j], attn_w_out[j], ln_mix_g[i], ln_mix_b[i],
                            batch=batch, seq=seq, alpha=alpha)
        h = _mlp_layer(h, *mlp_weights[i], ln_mlp_g[i], ln_mlp_b[i], alpha=alpha)
    return h.reshape(batch, seq, d)
```

```python
import functools
import math

import jax
import jax.numpy as jnp
from jax import lax
from jax.experimental import pallas as pl
from jax.experimental.pallas import tpu as pltpu

POOL_WINDOWS = (2, 4, 8, 16)
MAX_POOL_WINDOW = max(POOL_WINDOWS)
ATTN_PATTERNS = ((128, 1), (512, 4), (2048, 16))
HEAD_DIM = 128
ATTN_BLOCK = 128
LN_EPS = 1e-5
MASK_DISTANCE = 1e30
LOG2_E = math.log2(math.e)
LN_2 = math.log(2.0)

LANES = 128
BF16_SUBLANES = 16
V7X_VMEM_LIMIT_BYTES = 60 * 1024 * 1024
LN_ROW_CHUNK = 32
LN_UNROLL = 4
EPILOGUE_SLABS = 4
MLP_FF_BLOCK = 1024
MLP_SUB_CHUNK = 512
QKV_MAX_STRIDE = 4

_F32 = jnp.float32
_BF16 = jnp.bfloat16


def _params(*semantics):
    return pltpu.CompilerParams(dimension_semantics=semantics, vmem_limit_bytes=V7X_VMEM_LIMIT_BYTES)


def _resident(shape):
    return pl.BlockSpec(shape, lambda *_: (0,) * len(shape), pipeline_mode=pl.Buffered(1))


def _residual_layer_norm_rows(x_ref, h_ref, g_ref, b_ref, alpha, first_row, n_rows):
    gam = g_ref[...]
    bet = b_ref[...]
    group = LN_ROW_CHUNK * LN_UNROLL
    assert n_rows % group == 0
    for offset in range(0, n_rows, group):
        chunks = [pl.ds(first_row + offset + u * LN_ROW_CHUNK, LN_ROW_CHUNK) for u in range(LN_UNROLL)]
        rs = [alpha * x_ref[rows, :] + h_ref[rows, :] for rows in chunks]
        xcs = [r - jnp.mean(r, axis=-1, keepdims=True) for r in rs]
        rstds = [lax.rsqrt(jnp.mean(xc * xc, axis=-1, keepdims=True) + LN_EPS) for xc in xcs]
        for rows, xc, rstd in zip(chunks, xcs, rstds):
            h_ref[rows, :] = xc * rstd * gam + bet


def _cast_specs(w_stack, layer, n_blocks):
    _, rows, cols = w_stack.shape
    assert rows % (n_blocks * BF16_SUBLANES) == 0
    block = rows // n_blocks
    return (pl.BlockSpec((None, block, cols), lambda i, *_: (layer, i, 0)),
            pl.BlockSpec((block, cols), lambda i, *_: (i, 0)),
            jax.ShapeDtypeStruct((rows, cols), _BF16))


def _row_slabs(n_rows, n_slabs):
    assert n_rows % n_slabs == 0
    size = n_rows // n_slabs
    return [(s * size, size) for s in range(n_slabs)]


def _cast_blocks(src_refs, dst_refs):
    for src_ref, dst_ref in zip(src_refs, dst_refs):
        dst_ref[...] = src_ref[...].astype(dst_ref.dtype)


def _pool_in_kernel(*refs, blocks_per_seq, n_casts):
    x_ref, w_ref = refs[:2]
    p_ref = refs[2 + n_casts]
    carry_ref = refs[-1]
    _cast_blocks(refs[2:2 + n_casts], refs[3 + n_casts:-1])
    tm = x_ref.shape[0]
    gd = w_ref.shape[1] // len(POOL_WINDOWS)
    blk = pl.program_id(0) % blocks_per_seq

    @pl.when(blk == 0)
    def _():
        carry_ref[...] = jnp.zeros_like(carry_ref)

    xb = x_ref[...].astype(_BF16)
    pos = lax.broadcasted_iota(jnp.int32, (tm, 1), 0) + blk * tm
    for g, w in reversed(list(enumerate(POOL_WINDOWS))):
        cols = slice(g * gd, (g + 1) * gd)
        u = jnp.dot(xb, w_ref[:, cols], preferred_element_type=_F32)
        t = jnp.concatenate([carry_ref[g], u], axis=0)
        carry_ref[g] = u[tm - MAX_POOL_WINDOW:, :]
        k = 1
        while k < w:
            t = t + pltpu.roll(t, k, 0)
            k *= 2
        inv_cnt = 1.0 / jnp.minimum(pos + 1, w).astype(_F32)
        p_ref[:, cols] = (t[MAX_POOL_WINDOW:, :] * inv_cnt - u).astype(p_ref.dtype)


def _pool_out_kernel(*refs, alpha, n_casts):
    p_ref, x_ref, wg_ref, scale_ref, wo_ref, g_ref, b_ref = refs[:7]
    o_ref = refs[7 + n_casts]
    _cast_blocks(refs[7:7 + n_casts], refs[8 + n_casts:])
    n_groups, gd, _ = wg_ref.shape
    ys = []
    for g in range(n_groups):
        cols = slice(g * gd, (g + 1) * gd)
        y = jnp.dot(p_ref[:, cols], wg_ref[g], preferred_element_type=_F32) * scale_ref[:, cols]
        ys.append(y.astype(_BF16))
    yb = jnp.concatenate(ys, axis=1)
    for first, size in _row_slabs(x_ref.shape[0], EPILOGUE_SLABS):
        o_ref[first:first + size, :] = jnp.dot(yb[first:first + size, :], wo_ref[...], preferred_element_type=_F32)
        _residual_layer_norm_rows(x_ref, o_ref, g_ref, b_ref, alpha, first, size)


def _pool_layer(x, w_in, w_group, scale, w_out, ln_g, ln_b, casts_a, casts_b, *, seq, alpha):
    rows, d = x.shape
    n_groups, gd, _ = w_group.shape
    tm = min(512, seq)
    assert seq % tm == 0 and tm >= MAX_POOL_WINDOW and gd * n_groups == d
    row_block = pl.BlockSpec((tm, d), lambda i: (i, 0))
    specs_a = [_cast_specs(*c, rows // tm) for c in casts_a]
    specs_b = [_cast_specs(*c, rows // tm) for c in casts_b]
    p, *a_bf16 = pl.pallas_call(
        functools.partial(_pool_in_kernel, blocks_per_seq=seq // tm, n_casts=len(casts_a)),
        out_shape=(jax.ShapeDtypeStruct((rows, d), _BF16), *[s[2] for s in specs_a]),
        grid=(rows // tm,),
        in_specs=[row_block, _resident((d, d)), *[s[0] for s in specs_a]],
        out_specs=(row_block, *[s[1] for s in specs_a]),
        scratch_shapes=[pltpu.VMEM((n_groups, MAX_POOL_WINDOW, gd), _F32)],
        compiler_params=_params("arbitrary"),
        name="pool_in",
    )(x, w_in.astype(_BF16), *[c[0] for c in casts_a])
    h, *b_bf16 = pl.pallas_call(
        functools.partial(_pool_out_kernel, alpha=alpha, n_casts=len(casts_b)),
        out_shape=(jax.ShapeDtypeStruct((rows, d), _F32), *[s[2] for s in specs_b]),
        grid=(rows // tm,),
        in_specs=[row_block, row_block, _resident((n_groups, gd, gd)), _resident((1, d)),
                  _resident((d, d)), _resident((1, d)), _resident((1, d)), *[s[0] for s in specs_b]],
        out_specs=(row_block, *[s[1] for s in specs_b]),
        compiler_params=_params("parallel"),
        name="pool_out",
    )(p, x, w_group.astype(_BF16), scale.reshape(1, d), w_out.astype(_BF16),
      ln_g.reshape(1, d), ln_b.reshape(1, d), *[c[0] for c in casts_b])
    return h, a_bf16, b_bf16


def _mlp_kernel(x_ref, wu_ref, wd_ref, g_ref, b_ref, o_ref, xb_ref, *, alpha):
    j = pl.program_id(1)

    @pl.when(j == 0)
    def _():
        xb_ref[...] = x_ref[...].astype(_BF16)
        o_ref[...] = jnp.zeros_like(o_ref)

    for c in range(wu_ref.shape[1] // MLP_SUB_CHUNK):
        cols = slice(c * MLP_SUB_CHUNK, (c + 1) * MLP_SUB_CHUNK)
        h = jnp.maximum(jnp.dot(xb_ref[...], wu_ref[:, cols], preferred_element_type=_F32), 0.0)
        o_ref[...] += jnp.dot((h * h).astype(_BF16), wd_ref[cols, :], preferred_element_type=_F32)

    @pl.when(j == pl.num_programs(1) - 1)
    def _():
        group = LN_ROW_CHUNK * LN_UNROLL

        def step(c, carry):
            _residual_layer_norm_rows(x_ref, o_ref, g_ref, b_ref, alpha, pl.multiple_of(c * group, group), group)
            return carry

        lax.fori_loop(0, x_ref.shape[0] // group, step, 0)


def _mlp_layer(x, w_up, w_down, ln_g, ln_b, *, alpha):
    rows, d = x.shape
    d_ff = w_up.shape[1]
    tm = min(1024, rows)
    tf = min(MLP_FF_BLOCK, d_ff)
    assert rows % tm == 0 and d_ff % tf == 0 and tf % MLP_SUB_CHUNK == 0
    return pl.pallas_call(
        functools.partial(_mlp_kernel, alpha=alpha),
        out_shape=jax.ShapeDtypeStruct((rows, d), _F32),
        grid=(rows // tm, d_ff // tf),
        in_specs=[pl.BlockSpec((tm, d), lambda i, j: (i, 0)),
                  pl.BlockSpec((d, tf), lambda i, j: (0, j)),
                  pl.BlockSpec((tf, d), lambda i, j: (j, 0)),
                  _resident((1, d)), _resident((1, d))],
        out_specs=pl.BlockSpec((tm, d), lambda i, j: (i, 0)),
        scratch_shapes=[pltpu.VMEM((tm, d), _BF16)],
        compiler_params=_params("parallel", "arbitrary"),
        name="sqrelu_mlp",
    )(x, w_up, w_down, ln_g.reshape(1, d), ln_b.reshape(1, d))


def _deinterleave(res, o_ref, scratch, dil):
    if dil == 1:
        o_ref[0] = res.astype(o_ref.dtype)
        return
    res_ref = scratch[0]
    tm = res_ref.shape[1]
    n = tm // dil
    col_tiles = [slice(c * LANES, (c + 1) * LANES) for c in range(res_ref.shape[0])]
    for c, cols in enumerate(col_tiles):
        res_ref[c] = res[:, cols]
    if dil <= QKV_MAX_STRIDE:
        for r in range(dil):
            for c, cols in enumerate(col_tiles):
                o_ref[r, :, cols] = res_ref[c, pl.ds(r, n, stride=dil), :].astype(o_ref.dtype)
        return
    tmp_ref = scratch[1]
    s1 = QKV_MAX_STRIDE
    s2 = dil // s1
    for b in range(s1):
        for c in range(len(col_tiles)):
            tmp_ref[c, b * (tm // s1):(b + 1) * (tm // s1), :] = res_ref[c, pl.ds(b, tm // s1, stride=s1), :]
    for r in range(dil):
        for c, cols in enumerate(col_tiles):
            src = pl.ds((r % s1) * (tm // s1) + r // s1, n, stride=s2)
            o_ref[r, :, cols] = tmp_ref[c, src, :].astype(o_ref.dtype)


def _qkv_kernel(*refs, dil, emit_bf16):
    x_ref, w_refs, o_ref = refs[0], refs[1:4], refs[4]
    if emit_bf16:
        xb_ref, *scratch = refs[5:]
        xb_ref[...] = x_ref[...].astype(xb_ref.dtype)
    else:
        xb_ref, scratch = x_ref, refs[5:]
    for part, w_ref in enumerate(w_refs):
        res = jnp.dot(xb_ref[...], w_ref[...], preferred_element_type=_F32)
        _deinterleave(res, o_ref.at[part], scratch, dil)


def _qkv_group(x, w_qkv, *, batch, seq, group, n_groups, gw):
    rows, d = x.shape
    _, dil = ATTN_PATTERNS[group]
    emit_bf16 = x.dtype != _BF16
    tm = min(1024, seq)
    assert seq % tm == 0 and tm % (dil * BF16_SUBLANES) == 0
    assert dil <= QKV_MAX_STRIDE or dil % QKV_MAX_STRIDE == 0
    blocks_per_seq = seq // tm
    x_block = pl.BlockSpec((tm, d), lambda i: (i, 0))
    w_blocks = [pl.BlockSpec((d, gw), lambda i, c=part * n_groups + group: (0, c), pipeline_mode=pl.Buffered(1))
                for part in range(3)]
    qkv_shape = jax.ShapeDtypeStruct((batch, 3, dil, seq // dil, gw), _BF16)
    qkv_block = pl.BlockSpec((None, 3, dil, tm // dil, gw),
                             lambda i: (i // blocks_per_seq, 0, 0, i % blocks_per_seq, 0))
    outs = pl.pallas_call(
        functools.partial(_qkv_kernel, dil=dil, emit_bf16=emit_bf16),
        out_shape=(qkv_shape, jax.ShapeDtypeStruct((rows, d), _BF16)) if emit_bf16 else qkv_shape,
        grid=(rows // tm,),
        in_specs=[x_block, *w_blocks],
        out_specs=(qkv_block, x_block) if emit_bf16 else qkv_block,
        scratch_shapes=[pltpu.VMEM((gw // LANES, tm, LANES), _F32)] * ((dil > 1) + (dil > QKV_MAX_STRIDE)),
        compiler_params=_params("parallel"),
        name=f"qkv_proj_g{group}",
    )(x, w_qkv, w_qkv, w_qkv)
    return outs if emit_bf16 else (outs, x)


def _attn_kernel(slope_ref, q_ref, k_ref, v_ref, o_ref, lse_ref, osc_ref, bias_ref, *, dil, n_back):
    sub_len = q_ref.shape[1]
    n_heads = q_ref.shape[2] // HEAD_DIM
    nb = sub_len // ATTN_BLOCK
    nq = ATTN_BLOCK
    nk = bias_ref.shape[-1]
    lane = lax.broadcasted_iota(jnp.int32, (nq, LANES), 1)
    qk_scale = LOG2_E / math.sqrt(HEAD_DIM)
    cols = [slice(h * HEAD_DIM, (h + 1) * HEAD_DIM) for h in range(n_heads)]
    ones = jnp.ones((nk, HEAD_DIM), _BF16)

    @pl.when(pl.program_id(0) == 0)
    def _():
        q_minus_k = lax.broadcasted_iota(jnp.int32, (nq, nk), 0) - lax.broadcasted_iota(jnp.int32, (nq, nk), 1)
        for i in range(bias_ref.shape[0]):
            dist = q_minus_k + i * nq
            valid = (dist >= 0) & (dist <= n_back)
            neg_dist = jnp.where(valid, -(dist * dil).astype(_F32), -MASK_DISTANCE)
            for h in range(n_heads):
                bias_ref[i, h] = (slope_ref[h] * LOG2_E) * neg_dist

    def block(p, carry):
        r = p // nb
        q0 = pl.multiple_of((p % nb) * nq, nq)
        k0 = pl.multiple_of(jnp.maximum(q0 - nq, 0), nq) if nb > 1 else 0
        back = (q0 - k0) // nq
        tok = pl.ds(q0 * dil + r, nq, stride=dil) if dil > 1 else pl.ds(q0, nq)
        ss = [lax.dot_general(q_ref[r, pl.ds(q0, nq), c], k_ref[r, pl.ds(k0, nk), c], (((1,), (1,)), ((), ())),
                              preferred_element_type=_F32) * qk_scale + bias_ref[back, h]
              for h, c in enumerate(cols)]
        ms = [jnp.max(s, axis=-1, keepdims=True) for s in ss]
        es = [jnp.exp2(s - m).astype(_BF16) for s, m in zip(ss, ms)]
        os = [jnp.dot(e, jnp.concatenate([v_ref[r, pl.ds(k0, nk), c], ones], axis=1), preferred_element_type=_F32)
              for e, c in zip(es, cols)]
        m_all = jnp.zeros((nq, LANES), _F32)
        den_all = jnp.ones((nq, LANES), _F32)
        for h in range(n_heads):
            den = os[h][:, HEAD_DIM:]
            osc_ref.at[h][tok, :] = os[h][:, :HEAD_DIM] * (1.0 / den)
            m_all = jnp.where(lane == h, ms[h], m_all)
            den_all = jnp.where(lane == h, den, den_all)
        lse_ref[tok, :] = (m_all + jnp.log2(den_all)) * LN_2
        return carry

    lax.fori_loop(0, dil * nb, block, 0)
    for h in range(n_heads):
        o_ref[:, cols[h]] = osc_ref[h].astype(o_ref.dtype)


def _attn_group(qkv, slopes, *, batch, seq, group, gw):
    window, dil = ATTN_PATTERNS[group]
    sub_len = seq // dil
    assert window // dil == ATTN_BLOCK and sub_len % ATTN_BLOCK == 0 and gw // HEAD_DIM <= LANES == HEAD_DIM
    n_heads = gw // HEAD_DIM
    windows = 2 if sub_len > ATTN_BLOCK else 1

    def part(j):
        return pl.BlockSpec((None, None, dil, sub_len, gw), lambda b: (b, j, 0, 0, 0))

    return pl.pallas_call(
        functools.partial(_attn_kernel, dil=dil, n_back=window // dil),
        out_shape=(jax.ShapeDtypeStruct((batch * seq, gw), _BF16),
                   jax.ShapeDtypeStruct((batch * seq, LANES), _F32)),
        grid=(batch,),
        in_specs=[pl.BlockSpec(memory_space=pltpu.SMEM), part(0), part(1), part(2)],
        out_specs=(pl.BlockSpec((seq, gw), lambda b: (b, 0)), pl.BlockSpec((seq, LANES), lambda b: (b, 0))),
        scratch_shapes=[pltpu.VMEM((n_heads, seq, HEAD_DIM), _F32),
                        pltpu.VMEM((windows, n_heads, ATTN_BLOCK, windows * ATTN_BLOCK), _F32)],
        compiler_params=_params("arbitrary"),
        name=f"dilated_attn_g{group}",
    )(slopes, qkv, qkv, qkv)


def _attn_out_kernel(*refs, alpha, n_groups):
    o_refs = refs[:n_groups]
    l_refs = refs[n_groups:2 * n_groups]
    x_ref, w_ref, g_ref, b_ref, out_ref = refs[2 * n_groups:]
    gw = w_ref.shape[0]
    expand = (lax.broadcasted_iota(jnp.int32, (2 * LANES, gw), 1) // HEAD_DIM
              == lax.broadcasted_iota(jnp.int32, (2 * LANES, gw), 0) % LANES).astype(_BF16)

    lses = [l[...] for l in l_refs]
    m = functools.reduce(jnp.maximum, lses)
    es = [jnp.exp(l - m) for l in lses]
    inv = 1.0 / functools.reduce(jnp.add, es)
    acc = None
    for e, o_ref in zip(es, o_refs):
        w = e * inv
        w_hi = w.astype(_BF16)
        w_lo = (w - w_hi.astype(_F32)).astype(_BF16)
        wide = jnp.dot(jnp.concatenate([w_hi, w_lo], axis=1), expand, preferred_element_type=_F32)
        term = wide * o_ref[...].astype(_F32)
        acc = term if acc is None else acc + term
    merged = acc.astype(_BF16)
    for first, size in _row_slabs(x_ref.shape[0], EPILOGUE_SLABS):
        out_ref[first:first + size, :] = jnp.dot(merged[first:first + size, :], w_ref[...],
                                                 preferred_element_type=_F32)
        _residual_layer_norm_rows(x_ref, out_ref, g_ref, b_ref, alpha, first, size)


def _alibi_slopes(n_groups, n_slots):
    n = n_groups * n_slots
    return jnp.exp2(-8.0 * jnp.arange(1, n + 1, dtype=_F32) / n).reshape(n_groups, n_slots)


def _attn_layer(x, w_qkv, w_out, ln_g, ln_b, *, batch, seq, alpha):
    rows, d = x.shape
    n_groups = len(ATTN_PATTERNS)
    gw = w_out.shape[0]
    assert w_qkv.shape[1] == 3 * n_groups * gw
    w_qkv = w_qkv.astype(_BF16)
    slopes = _alibi_slopes(n_groups, gw // HEAD_DIM)
    outs, lses = [], []
    x_mxu = x
    for g in range(n_groups):
        qkv, x_mxu = _qkv_group(x_mxu, w_qkv, batch=batch, seq=seq, group=g, n_groups=n_groups, gw=gw)
        o, lse = _attn_group(qkv, slopes[g], batch=batch, seq=seq, group=g, gw=gw)
        outs.append(o)
        lses.append(lse)
    tm = min(512, rows)
    head_block = pl.BlockSpec((tm, gw), lambda i: (i, 0))
    lse_block = pl.BlockSpec((tm, LANES), lambda i: (i, 0))
    row_block = pl.BlockSpec((tm, d), lambda i: (i, 0))
    return pl.pallas_call(
        functools.partial(_attn_out_kernel, alpha=alpha, n_groups=n_groups),
        out_shape=jax.ShapeDtypeStruct((rows, d), _F32),
        grid=(rows // tm,),
        in_specs=[head_block] * n_groups + [lse_block] * n_groups
        + [row_block, _resident((gw, d)), _resident((1, d)), _resident((1, d))],
        out_specs=row_block,
        compiler_params=_params("parallel"),
        name="attn_merge_out",
    )(*outs, *lses, x, w_out.astype(_BF16), ln_g.reshape(1, d), ln_b.reshape(1, d))


def kernel(x, pool_w_in, pool_w_group, pool_scale, pool_w_out, attn_w_qkv, attn_w_out, mlp_w_up, mlp_w_down, ln_mix_g, ln_mix_b, ln_mlp_g, ln_mlp_b):
    batch, seq, d = x.shape
    depth = mlp_w_up.shape[0]
    alpha = (2 * depth) ** 0.25
    h = x.reshape(batch * seq, d)
    mlp_weights = {}
    for i in range(depth):
        j = i // 2
        if i % 2 == 0:
            layers = [l for l in (i, i + 1) if l < depth]
            h, ups, downs = _pool_layer(h, pool_w_in[j], pool_w_group[j], pool_scale[j], pool_w_out[j],
                                        ln_mix_g[i], ln_mix_b[i], [(mlp_w_up, l) for l in layers],
                                        [(mlp_w_down, l) for l in layers], seq=seq, alpha=alpha)
            mlp_weights.update(zip(layers, zip(ups, downs)))
        else:
            h = _attn_layer(h, attn_w_qkv[j], attn_w_out[j], ln_mix_g[i], ln_mix_b[i],
                            batch=batch, seq=seq, alpha=alpha)
        h = _mlp_layer(h, *mlp_weights[i], ln_mlp_g[i], ln_mlp_b[i], alpha=alpha)
    return h.reshape(batch, seq, d)
```

```python
import functools
import math

import jax
import jax.numpy as jnp
from jax import lax
from jax.experimental import pallas as pl
from jax.experimental.pallas import tpu as pltpu

POOL_WINDOWS = (2, 4, 8, 16)
MAX_POOL_WINDOW = max(POOL_WINDOWS)
ATTN_PATTERNS = ((128, 1), (512, 4), (2048, 16))
HEAD_DIM = 128
ATTN_BLOCK = 128
LN_EPS = 1e-5
MASK_DISTANCE = 1e30
LOG2_E = math.log2(math.e)
LN_2 = math.log(2.0)

LANES = 128
BF16_SUBLANES = 16
V7X_VMEM_LIMIT_BYTES = 60 * 1024 * 1024
LN_ROW_CHUNK = 32
LN_UNROLL = 4
EPILOGUE_SLABS = 4
MLP_FF_BLOCK = 1024
MLP_SUB_CHUNK = 512
QKV_MAX_STRIDE = 4

_F32 = jnp.float32
_BF16 = jnp.bfloat16


def _params(*semantics):
    return pltpu.CompilerParams(dimension_semantics=semantics, vmem_limit_bytes=V7X_VMEM_LIMIT_BYTES)


def _resident(shape):
    return pl.BlockSpec(shape, lambda *_: (0,) * len(shape), pipeline_mode=pl.Buffered(1))


def _residual_layer_norm_rows(x_ref, h_ref, g_ref, b_ref, alpha, first_row, n_rows):
    gam = g_ref[...]
    bet = b_ref[...]
    group = LN_ROW_CHUNK * LN_UNROLL
    assert n_rows % group == 0
    for offset in range(0, n_rows, group):
        chunks = [pl.ds(first_row + offset + u * LN_ROW_CHUNK, LN_ROW_CHUNK) for u in range(LN_UNROLL)]
        rs = [alpha * x_ref[rows, :] + h_ref[rows, :] for rows in chunks]
        xcs = [r - jnp.mean(r, axis=-1, keepdims=True) for r in rs]
        rstds = [lax.rsqrt(jnp.mean(xc * xc, axis=-1, keepdims=True) + LN_EPS) for xc in xcs]
        for rows, xc, rstd in zip(chunks, xcs, rstds):
            h_ref[rows, :] = xc * rstd * gam + bet


def _cast_specs(w_stack, layer, n_blocks):
    _, rows, cols = w_stack.shape
    assert rows % (n_blocks * BF16_SUBLANES) == 0
    block = rows // n_blocks
    return (pl.BlockSpec((None, block, cols), lambda i, *_: (layer, i, 0)),
            pl.BlockSpec((block, cols), lambda i, *_: (i, 0)),
            jax.ShapeDtypeStruct((rows, cols), _BF16))


def _row_slabs(n_rows, n_slabs):
    assert n_rows % n_slabs == 0
    size = n_rows // n_slabs
    return [(s * size, size) for s in range(n_slabs)]


def _cast_blocks(src_refs, dst_refs):
    for src_ref, dst_ref in zip(src_refs, dst_refs):
        dst_ref[...] = src_ref[...].astype(dst_ref.dtype)


def _pool_in_kernel(*refs, blocks_per_seq, n_casts):
    x_ref, w_ref = refs[:2]
    p_ref = refs[2 + n_casts]
    carry_ref = refs[-1]
    _cast_blocks(refs[2:2 + n_casts], refs[3 + n_casts:-1])
    tm = x_ref.shape[0]
    gd = w_ref.shape[1] // len(POOL_WINDOWS)
    blk = pl.program_id(0) % blocks_per_seq

    @pl.when(blk == 0)
    def _():
        carry_ref[...] = jnp.zeros_like(carry_ref)

    xb = x_ref[...].astype(_BF16)
    pos = lax.broadcasted_iota(jnp.int32, (tm, 1), 0) + blk * tm
    for g, w in reversed(list(enumerate(POOL_WINDOWS))):
        cols = slice(g * gd, (g + 1) * gd)
        u = jnp.dot(xb, w_ref[:, cols], preferred_element_type=_F32)
        t = jnp.concatenate([carry_ref[g], u], axis=0)
        carry_ref[g] = u[tm - MAX_POOL_WINDOW:, :]
        k = 1
        while k < w:
            t = t + pltpu.roll(t, k, 0)
            k *= 2
        inv_cnt = 1.0 / jnp.minimum(pos + 1, w).astype(_F32)
        p_ref[:, cols] = (t[MAX_POOL_WINDOW:, :] * inv_cnt - u).astype(p_ref.dtype)


def _pool_out_kernel(*refs, alpha, n_casts):
    p_ref, x_ref, wg_ref, scale_ref, wo_ref, g_ref, b_ref = refs[:7]
    o_ref = refs[7 + n_casts]
    _cast_blocks(refs[7:7 + n_casts], refs[8 + n_casts:])
    n_groups, gd, _ = wg_ref.shape
    ys = []
    for g in range(n_groups):
        cols = slice(g * gd, (g + 1) * gd)
        y = jnp.dot(p_ref[:, cols], wg_ref[g], preferred_element_type=_F32) * scale_ref[:, cols]
        ys.append(y.astype(_BF16))
    yb = jnp.concatenate(ys, axis=1)
    for first, size in _row_slabs(x_ref.shape[0], EPILOGUE_SLABS):
        o_ref[first:first + size, :] = jnp.dot(yb[first:first + size, :], wo_ref[...], preferred_element_type=_F32)
        _residual_layer_norm_rows(x_ref, o_ref, g_ref, b_ref, alpha, first, size)


def _pool_layer(x, w_in, w_group, scale, w_out, ln_g, ln_b, casts_a, casts_b, *, seq, alpha):
    rows, d = x.shape
    n_groups, gd, _ = w_group.shape
    tm = min(512, seq)
    assert seq % tm == 0 and tm >= MAX_POOL_WINDOW and gd * n_groups == d
    row_block = pl.BlockSpec((tm, d), lambda i: (i, 0))
    specs_a = [_cast_specs(*c, rows // tm) for c in casts_a]
    specs_b = [_cast_specs(*c, rows // tm) for c in casts_b]
    p, *a_bf16 = pl.pallas_call(
        functools.partial(_pool_in_kernel, blocks_per_seq=seq // tm, n_casts=len(casts_a)),
        out_shape=(jax.ShapeDtypeStruct((rows, d), _BF16), *[s[2] for s in specs_a]),
        grid=(rows // tm,),
        in_specs=[row_block, _resident((d, d)), *[s[0] for s in specs_a]],
        out_specs=(row_block, *[s[1] for s in specs_a]),
        scratch_shapes=[pltpu.VMEM((n_groups, MAX_POOL_WINDOW, gd), _F32)],
        compiler_params=_params("arbitrary"),
        name="pool_in",
    )(x, w_in.astype(_BF16), *[c[0] for c in casts_a])
    h, *b_bf16 = pl.pallas_call(
        functools.partial(_pool_out_kernel, alpha=alpha, n_casts=len(casts_b)),
        out_shape=(jax.ShapeDtypeStruct((rows, d), _F32), *[s[2] for s in specs_b]),
        grid=(rows // tm,),
        in_specs=[row_block, row_block, _resident((n_groups, gd, gd)), _resident((1, d)),
                  _resident((d, d)), _resident((1, d)), _resident((1, d)), *[s[0] for s in specs_b]],
        out_specs=(row_block, *[s[1] for s in specs_b]),
        compiler_params=_params("parallel"),
        name="pool_out",
    )(p, x, w_group.astype(_BF16), scale.reshape(1, d), w_out.astype(_BF16),
      ln_g.reshape(1, d), ln_b.reshape(1, d), *[c[0] for c in casts_b])
    return h, a_bf16, b_bf16


def _mlp_kernel(x_ref, wu_ref, wd_ref, g_ref, b_ref, o_ref, xb_ref, *, alpha):
    j = pl.program_id(1)

    @pl.when(j == 0)
    def _():
        xb_ref[...] = x_ref[...].astype(_BF16)
        o_ref[...] = jnp.zeros_like(o_ref)

    for c in range(wu_ref.shape[1] // MLP_SUB_CHUNK):
        cols = slice(c * MLP_SUB_CHUNK, (c + 1) * MLP_SUB_CHUNK)
        h = jnp.maximum(jnp.dot(xb_ref[...], wu_ref[:, cols], preferred_element_type=_F32), 0.0)
        o_ref[...] += jnp.dot((h * h).astype(_BF16), wd_ref[cols, :], preferred_element_type=_F32)

    @pl.when(j == pl.num_programs(1) - 1)
    def _():
        group = LN_ROW_CHUNK * LN_UNROLL

        def step(c, carry):
            _residual_layer_norm_rows(x_ref, o_ref, g_ref, b_ref, alpha, pl.multiple_of(c * group, group), group)
            return carry

        lax.fori_loop(0, x_ref.shape[0] // group, step, 0)


def _mlp_layer(x, w_up, w_down, ln_g, ln_b, *, alpha):
    rows, d = x.shape
    d_ff = w_up.shape[1]
    tm = min(1024, rows)
    tf = min(MLP_FF_BLOCK, d_ff)
    assert rows % tm == 0 and d_ff % tf == 0 and tf % MLP_SUB_CHUNK == 0
    return pl.pallas_call(
        functools.partial(_mlp_kernel, alpha=alpha),
        out_shape=jax.ShapeDtypeStruct((rows, d), _F32),
        grid=(rows // tm, d_ff // tf),
        in_specs=[pl.BlockSpec((tm, d), lambda i, j: (i, 0)),
                  pl.BlockSpec((d, tf), lambda i, j: (0, j)),
                  pl.BlockSpec((tf, d), lambda i, j: (j, 0)),
                  _resident((1, d)), _resident((1, d))],
        out_specs=pl.BlockSpec((tm, d), lambda i, j: (i, 0)),
        scratch_shapes=[pltpu.VMEM((tm, d), _BF16)],
        compiler_params=_params("parallel", "arbitrary"),
        name="sqrelu_mlp",
    )(x, w_up, w_down, ln_g.reshape(1, d), ln_b.reshape(1, d))


def _deinterleave(res, o_ref, scratch, dil):
    if dil == 1:
        o_ref[0] = res.astype(o_ref.dtype)
        return
    res_ref = scratch[0]
    tm = res_ref.shape[1]
    n = tm // dil
    col_tiles = [slice(c * LANES, (c + 1) * LANES) for c in range(res_ref.shape[0])]
    for c, cols in enumerate(col_tiles):
        res_ref[c] = res[:, cols]
    if dil <= QKV_MAX_STRIDE:
        for r in range(dil):
            for c, cols in enumerate(col_tiles):
                o_ref[r, :, cols] = res_ref[c, pl.ds(r, n, stride=dil), :].astype(o_ref.dtype)
        return
    tmp_ref = scratch[1]
    s1 = QKV_MAX_STRIDE
    s2 = dil // s1
    for b in range(s1):
        for c in range(len(col_tiles)):
            tmp_ref[c, b * (tm // s1):(b + 1) * (tm // s1), :] = res_ref[c, pl.ds(b, tm // s1, stride=s1), :]
    for r in range(dil):
        for c, cols in enumerate(col_tiles):
            src = pl.ds((r % s1) * (tm // s1) + r // s1, n, stride=s2)
            o_ref[r, :, cols] = tmp_ref[c, src, :].astype(o_ref.dtype)


def _qkv_kernel(*refs, dil, emit_bf16):
    x_ref, w_refs, o_ref = refs[0], refs[1:4], refs[4]
    if emit_bf16:
        xb_ref, *scratch = refs[5:]
        xb_ref[...] = x_ref[...].astype(xb_ref.dtype)
    else:
        xb_ref, scratch = x_ref, refs[5:]
    for part, w_ref in enumerate(w_refs):
        res = jnp.dot(xb_ref[...], w_ref[...], preferred_element_type=_F32)
        _deinterleave(res, o_ref.at[part], scratch, dil)


def _qkv_group(x, w_qkv, *, batch, seq, group, n_groups, gw):
    rows, d = x.shape
    _, dil = ATTN_PATTERNS[group]
    emit_bf16 = x.dtype != _BF16
    tm = min(1024, seq)
    assert seq % tm == 0 and tm % (dil * BF16_SUBLANES) == 0
    assert dil <= QKV_MAX_STRIDE or dil % QKV_MAX_STRIDE == 0
    blocks_per_seq = seq // tm
    x_block = pl.BlockSpec((tm, d), lambda i: (i, 0))
    w_blocks = [pl.BlockSpec((d, gw), lambda i, c=part * n_groups + group: (0, c), pipeline_mode=pl.Buffered(1))
                for part in range(3)]
    qkv_shape = jax.ShapeDtypeStruct((batch, 3, dil, seq // dil, gw), _BF16)
    qkv_block = pl.BlockSpec((None, 3, dil, tm // dil, gw),
                             lambda i: (i // blocks_per_seq, 0, 0, i % blocks_per_seq, 0))
    outs = pl.pallas_call(
        functools.partial(_qkv_kernel, dil=dil, emit_bf16=emit_bf16),
        out_shape=(qkv_shape, jax.ShapeDtypeStruct((rows, d), _BF16)) if emit_bf16 else qkv_shape,
        grid=(rows // tm,),
        in_specs=[x_block, *w_blocks],
        out_specs=(qkv_block, x_block) if emit_bf16 else qkv_block,
        scratch_shapes=[pltpu.VMEM((gw // LANES, tm, LANES), _F32)] * ((dil > 1) + (dil > QKV_MAX_STRIDE)),
        compiler_params=_params("parallel"),
        name=f"qkv_proj_g{group}",
    )(x, w_qkv, w_qkv, w_qkv)
    return outs if emit_bf16 else (outs, x)


def _attn_kernel(slope_ref, q_ref, k_ref, v_ref, o_ref, lse_ref, osc_ref, bias_ref, *, dil, n_back):
    sub_len = q_ref.shape[1]
    n_heads = q_ref.shape[2] // HEAD_DIM
    nb = sub_len // ATTN_BLOCK
    nq = ATTN_BLOCK
    nk = bias_ref.shape[-1]
    lane = lax.broadcasted_iota(jnp.int32, (nq, LANES), 1)
    qk_scale = LOG2_E / math.sqrt(HEAD_DIM)
    cols = [slice(h * HEAD_DIM, (h + 1) * HEAD_DIM) for h in range(n_heads)]
    ones = jnp.ones((nk, HEAD_DIM), _BF16)

    @pl.when(pl.program_id(0) == 0)
    def _():
        q_minus_k = lax.broadcasted_iota(jnp.int32, (nq, nk), 0) - lax.broadcasted_iota(jnp.int32, (nq, nk), 1)
        for i in range(bias_ref.shape[0]):
            dist = q_minus_k + i * nq
            valid = (dist >= 0) & (dist <= n_back)
            neg_dist = jnp.where(valid, -(dist * dil).astype(_F32), -MASK_DISTANCE)
            for h in range(n_heads):
                bias_ref[i, h] = (slope_ref[h] * LOG2_E) * neg_dist

    def block(p, carry):
        r = p // nb
        q0 = pl.multiple_of((p % nb) * nq, nq)
        k0 = pl.multiple_of(jnp.maximum(q0 - nq, 0), nq) if nb > 1 else 0
        back = (q0 - k0) // nq
        tok = pl.ds(q0 * dil + r, nq, stride=dil) if dil > 1 else pl.ds(q0, nq)
        ss = [lax.dot_general(q_ref[r, pl.ds(q0, nq), c], k_ref[r, pl.ds(k0, nk), c], (((1,), (1,)), ((), ())),
                              preferred_element_type=_F32) * qk_scale + bias_ref[back, h]
              for h, c in enumerate(cols)]
        ms = [jnp.max(s, axis=-1, keepdims=True) for s in ss]
        es = [jnp.exp2(s - m).astype(_BF16) for s, m in zip(ss, ms)]
        os = [jnp.dot(e, jnp.concatenate([v_ref[r, pl.ds(k0, nk), c], ones], axis=1), preferred_element_type=_F32)
              for e, c in zip(es, cols)]
        m_all = jnp.zeros((nq, LANES), _F32)
        den_all = jnp.ones((nq, LANES), _F32)
        for h in range(n_heads):
            den = os[h][:, HEAD_DIM:]
            osc_ref.at[h][tok, :] = os[h][:, :HEAD_DIM] * (1.0 / den)
            m_all = jnp.where(lane == h, ms[h], m_all)
            den_all = jnp.where(lane == h, den, den_all)
        lse_ref[tok, :] = (m_all + jnp.log2(den_all)) * LN_2
        return carry

    lax.fori_loop(0, dil * nb, block, 0)
    for h in range(n_heads):
        o_ref[:, cols[h]] = osc_ref[h].astype(o_ref.dtype)


def _attn_group(qkv, slopes, *, batch, seq, group, gw):
    window, dil = ATTN_PATTERNS[group]
    sub_len = seq // dil
    assert window // dil == ATTN_BLOCK and sub_len % ATTN_BLOCK == 0 and gw // HEAD_DIM <= LANES == HEAD_DIM
    n_heads = gw // HEAD_DIM
    windows = 2 if sub_len > ATTN_BLOCK else 1

    def part(j):
        return pl.BlockSpec((None, None, dil, sub_len, gw), lambda b: (b, j, 0, 0, 0))

    return pl.pallas_call(
        functools.partial(_attn_kernel, dil=dil, n_back=window // dil),
        out_shape=(jax.ShapeDtypeStruct((batch * seq, gw), _BF16),
                   jax.ShapeDtypeStruct((batch * seq, LANES), _F32)),
        grid=(batch,),
        in_specs=[pl.BlockSpec(memory_space=pltpu.SMEM), part(0), part(1), part(2)],
        out_specs=(pl.BlockSpec((seq, gw), lambda b: (b, 0)), pl.BlockSpec((seq, LANES), lambda b: (b, 0))),
        scratch_shapes=[pltpu.VMEM((n_heads, seq, HEAD_DIM), _F32),
                        pltpu.VMEM((windows, n_heads, ATTN_BLOCK, windows * ATTN_BLOCK), _F32)],
        compiler_params=_params("arbitrary"),
        name=f"dilated_attn_g{group}",
    )(slopes, qkv, qkv, qkv)


def _attn_out_kernel(*refs, alpha, n_groups):
    o_refs = refs[:n_groups]
    l_refs = refs[n_groups:2 * n_groups]
    x_ref, w_ref, g_ref, b_ref, out_ref = refs[2 * n_groups:]
    gw = w_ref.shape[0]
    expand = (lax.broadcasted_iota(jnp.int32, (2 * LANES, gw), 1) // HEAD_DIM
              == lax.broadcasted_iota(jnp.int32, (2 * LANES, gw), 0) % LANES).astype(_BF16)

    lses = [l[...] for l in l_refs]
    m = functools.reduce(jnp.maximum, lses)
    es = [jnp.exp(l - m) for l in lses]
    inv = 1.0 / functools.reduce(jnp.add, es)
    acc = None
    for e, o_ref in zip(es, o_refs):
        w = e * inv
        w_hi = w.astype(_BF16)
        w_lo = (w - w_hi.astype(_F32)).astype(_BF16)
        wide = jnp.dot(jnp.concatenate([w_hi, w_lo], axis=1), expand, preferred_element_type=_F32)
        term = wide * o_ref[...].astype(_F32)
        acc = term if acc is None else acc + term
    merged = acc.astype(_BF16)
    for first, size in _row_slabs(x_ref.shape[0], EPILOGUE_SLABS):
        out_ref[first:first + size, :] = jnp.dot(merged[first:first + size, :], w_ref[...],
                                                 preferred_element_type=_F32)
        _residual_layer_norm_rows(x_ref, out_ref, g_ref, b_ref, alpha, first, size)


def _alibi_slopes(n_groups, n_slots):
    n = n_groups * n_slots
    return jnp.exp2(-8.0 * jnp.arange(1, n + 1, dtype=_F32) / n).reshape(n_groups, n_slots)


def _attn_layer(x, w_qkv, w_out, ln_g, ln_b, *, batch, seq, alpha):
    rows, d = x.shape
    n_groups = len(ATTN_PATTERNS)
    gw = w_out.shape[0]
    assert w_qkv.shape[1] == 3 * n_groups * gw
    slopes = _alibi_slopes(n_groups, gw // HEAD_DIM)
    outs, lses = [], []
    x_mxu = x
    for g in range(n_groups):
        qkv, x_mxu = _qkv_group(x_mxu, w_qkv, batch=batch, seq=seq, group=g, n_groups=n_groups, gw=gw)
        o, lse = _attn_group(qkv, slopes[g], batch=batch, seq=seq, group=g, gw=gw)
        outs.append(o)
        lses.append(lse)
    tm = min(512, rows)
    head_block = pl.BlockSpec((tm, gw), lambda i: (i, 0))
    lse_block = pl.BlockSpec((tm, LANES), lambda i: (i, 0))
    row_block = pl.BlockSpec((tm, d), lambda i: (i, 0))
    return pl.pallas_call(
        functools.partial(_attn_out_kernel, alpha=alpha, n_groups=n_groups),
        out_shape=jax.ShapeDtypeStruct((rows, d), _F32),
        grid=(rows // tm,),
        in_specs=[head_block] * n_groups + [lse_block] * n_groups
        + [row_block, _resident((gw, d)), _resident((1, d)), _resident((1, d))],
        out_specs=row_block,
        compiler_params=_params("parallel"),
        name="attn_merge_out",
    )(*outs, *lses, x, w_out.astype(_BF16), ln_g.reshape(1, d), ln_b.reshape(1, d))


def kernel(x, pool_w_in, pool_w_group, pool_scale, pool_w_out, attn_w_qkv, attn_w_out, mlp_w_up, mlp_w_down, ln_mix_g, ln_mix_b, ln_mlp_g, ln_mlp_b):
    batch, seq, d = x.shape
    depth = mlp_w_up.shape[0]
    alpha = (2 * depth) ** 0.25
    h = x.reshape(batch * seq, d)
    mlp_weights = {}
    w_qkv = None
    for i in range(depth):
        j = i // 2
        if i % 2 == 0:
            layers = [l for l in (i, i + 1) if l < depth]
            casts_a = [(mlp_w_up, l) for l in layers] + [(attn_w_qkv, j)] * (i + 1 < depth)
            h, ups, downs = _pool_layer(h, pool_w_in[j], pool_w_group[j], pool_scale[j], pool_w_out[j],
                                        ln_mix_g[i], ln_mix_b[i], casts_a,
                                        [(mlp_w_down, l) for l in layers], seq=seq, alpha=alpha)
            mlp_weights.update(zip(layers, zip(ups, downs)))
            w_qkv = ups[-1] if i + 1 < depth else None
        else:
            h = _attn_layer(h, w_qkv, attn_w_out[j], ln_mix_g[i], ln_mix_b[i], batch=batch, seq=seq, alpha=alpha)
        h = _mlp_layer(h, *mlp_weights[i], ln_mlp_g[i], ln_mlp_b[i], alpha=alpha)
    return h.reshape(batch, seq, d)
```

```python
import functools
import math

import jax
import jax.numpy as jnp
from jax import lax
from jax.experimental import pallas as pl
from jax.experimental.pallas import tpu as pltpu

POOL_WINDOWS = (2, 4, 8, 16)
MAX_POOL_WINDOW = max(POOL_WINDOWS)
ATTN_PATTERNS = ((128, 1), (512, 4), (2048, 16))
HEAD_DIM = 128
ATTN_BLOCK = 128
LN_EPS = 1e-5
MASK_DISTANCE = 1e30
LOG2_E = math.log2(math.e)
LN_2 = math.log(2.0)

LANES = 128
BF16_SUBLANES = 16
V7X_VMEM_LIMIT_BYTES = 60 * 1024 * 1024
LN_ROW_CHUNK = 32
LN_UNROLL = 4
EPILOGUE_SLABS = 4
POOL_ROW_BLOCK = 512
MLP_ROW_BLOCK = 1024
QKV_ROW_BLOCK = 1024
MERGE_ROW_BLOCK = 512
MLP_FF_BLOCK = 1024
MLP_SUB_CHUNK = 512
QKV_MAX_STRIDE = 4

_F32 = jnp.float32
_BF16 = jnp.bfloat16


def _params(*semantics):
    return pltpu.CompilerParams(dimension_semantics=semantics, vmem_limit_bytes=V7X_VMEM_LIMIT_BYTES)


def _resident(shape):
    return pl.BlockSpec(shape, lambda *_: (0,) * len(shape), pipeline_mode=pl.Buffered(1))


def _residual_layer_norm_rows(x_ref, h_ref, g_ref, b_ref, alpha, first_row, n_rows):
    gam = g_ref[...]
    bet = b_ref[...]
    group = LN_ROW_CHUNK * LN_UNROLL
    assert n_rows % group == 0
    for offset in range(0, n_rows, group):
        chunks = [pl.ds(first_row + offset + u * LN_ROW_CHUNK, LN_ROW_CHUNK) for u in range(LN_UNROLL)]
        rs = [alpha * x_ref[rows, :] + h_ref[rows, :] for rows in chunks]
        xcs = [r - jnp.mean(r, axis=-1, keepdims=True) for r in rs]
        rstds = [lax.rsqrt(jnp.mean(xc * xc, axis=-1, keepdims=True) + LN_EPS) for xc in xcs]
        for rows, xc, rstd in zip(chunks, xcs, rstds):
            h_ref[rows, :] = xc * rstd * gam + bet


def _cast_specs(w_stack, layer, n_blocks):
    _, rows, cols = w_stack.shape
    assert rows % (n_blocks * BF16_SUBLANES) == 0
    block = rows // n_blocks
    return (pl.BlockSpec((None, block, cols), lambda i, *_: (layer, i, 0)),
            pl.BlockSpec((block, cols), lambda i, *_: (i, 0)),
            jax.ShapeDtypeStruct((rows, cols), _BF16))


def _row_slabs(n_rows, n_slabs):
    assert n_rows % n_slabs == 0
    size = n_rows // n_slabs
    return [(s * size, size) for s in range(n_slabs)]


def _cast_blocks(src_refs, dst_refs):
    for src_ref, dst_ref in zip(src_refs, dst_refs):
        dst_ref[...] = src_ref[...].astype(dst_ref.dtype)


def _pool_in_kernel(*refs, blocks_per_seq, n_casts):
    x_ref, w_ref = refs[:2]
    p_ref = refs[2 + n_casts]
    carry_ref = refs[-1]
    _cast_blocks(refs[2:2 + n_casts], refs[3 + n_casts:-1])
    tm = x_ref.shape[0]
    gd = w_ref.shape[1] // len(POOL_WINDOWS)
    blk = pl.program_id(0) % blocks_per_seq

    @pl.when(blk == 0)
    def _():
        carry_ref[...] = jnp.zeros_like(carry_ref)

    xb = x_ref[...].astype(_BF16)
    pos = lax.broadcasted_iota(jnp.int32, (tm, 1), 0) + blk * tm
    for g, w in reversed(list(enumerate(POOL_WINDOWS))):
        cols = slice(g * gd, (g + 1) * gd)
        u = jnp.dot(xb, w_ref[:, cols], preferred_element_type=_F32)
        t = jnp.concatenate([carry_ref[g], u], axis=0)
        carry_ref[g] = u[tm - MAX_POOL_WINDOW:, :]
        k = 1
        while k < w:
            t = t + pltpu.roll(t, k, 0)
            k *= 2
        inv_cnt = 1.0 / jnp.minimum(pos + 1, w).astype(_F32)
        p_ref[:, cols] = (t[MAX_POOL_WINDOW:, :] * inv_cnt - u).astype(p_ref.dtype)


def _pool_out_kernel(*refs, alpha, n_casts):
    p_ref, x_ref, wg_ref, scale_ref, wo_ref, g_ref, b_ref = refs[:7]
    o_ref = refs[7 + n_casts]
    _cast_blocks(refs[7:7 + n_casts], refs[8 + n_casts:])
    n_groups, gd, _ = wg_ref.shape
    ys = []
    for g in range(n_groups):
        cols = slice(g * gd, (g + 1) * gd)
        y = jnp.dot(p_ref[:, cols], wg_ref[g], preferred_element_type=_F32) * scale_ref[:, cols]
        ys.append(y.astype(_BF16))
    yb = jnp.concatenate(ys, axis=1)
    for first, size in _row_slabs(x_ref.shape[0], EPILOGUE_SLABS):
        o_ref[first:first + size, :] = jnp.dot(yb[first:first + size, :], wo_ref[...], preferred_element_type=_F32)
        _residual_layer_norm_rows(x_ref, o_ref, g_ref, b_ref, alpha, first, size)


def _pool_layer(x, w_in, w_group, scale, w_out, ln_g, ln_b, casts_a, casts_b, *, seq, alpha):
    rows, d = x.shape
    n_groups, gd, _ = w_group.shape
    tm = min(POOL_ROW_BLOCK, seq)
    assert seq % tm == 0 and tm >= MAX_POOL_WINDOW and gd * n_groups == d
    row_block = pl.BlockSpec((tm, d), lambda i: (i, 0))
    specs_a = [_cast_specs(*c, rows // tm) for c in casts_a]
    specs_b = [_cast_specs(*c, rows // tm) for c in casts_b]
    p, *a_bf16 = pl.pallas_call(
        functools.partial(_pool_in_kernel, blocks_per_seq=seq // tm, n_casts=len(casts_a)),
        out_shape=(jax.ShapeDtypeStruct((rows, d), _BF16), *[s[2] for s in specs_a]),
        grid=(rows // tm,),
        in_specs=[row_block, _resident((d, d)), *[s[0] for s in specs_a]],
        out_specs=(row_block, *[s[1] for s in specs_a]),
        scratch_shapes=[pltpu.VMEM((n_groups, MAX_POOL_WINDOW, gd), _F32)],
        compiler_params=_params("arbitrary"),
        name="pool_in",
    )(x, w_in.astype(_BF16), *[c[0] for c in casts_a])
    h, *b_bf16 = pl.pallas_call(
        functools.partial(_pool_out_kernel, alpha=alpha, n_casts=len(casts_b)),
        out_shape=(jax.ShapeDtypeStruct((rows, d), _F32), *[s[2] for s in specs_b]),
        grid=(rows // tm,),
        in_specs=[row_block, row_block, _resident((n_groups, gd, gd)), _resident((1, d)),
                  _resident((d, d)), _resident((1, d)), _resident((1, d)), *[s[0] for s in specs_b]],
        out_specs=(row_block, *[s[1] for s in specs_b]),
        compiler_params=_params("parallel"),
        name="pool_out",
    )(p, x, w_group.astype(_BF16), scale.reshape(1, d), w_out.astype(_BF16),
      ln_g.reshape(1, d), ln_b.reshape(1, d), *[c[0] for c in casts_b])
    return h, a_bf16, b_bf16


def _mlp_kernel(x_ref, wu_ref, wd_ref, g_ref, b_ref, o_ref, xb_ref, *, alpha):
    j = pl.program_id(1)

    @pl.when(j == 0)
    def _():
        xb_ref[...] = x_ref[...].astype(_BF16)
        o_ref[...] = jnp.zeros_like(o_ref)

    for c in range(wu_ref.shape[1] // MLP_SUB_CHUNK):
        cols = slice(c * MLP_SUB_CHUNK, (c + 1) * MLP_SUB_CHUNK)
        h = jnp.maximum(jnp.dot(xb_ref[...], wu_ref[:, cols], preferred_element_type=_F32), 0.0)
        o_ref[...] += jnp.dot((h * h).astype(_BF16), wd_ref[cols, :], preferred_element_type=_F32)

    @pl.when(j == pl.num_programs(1) - 1)
    def _():
        group = LN_ROW_CHUNK * LN_UNROLL

        def step(c, carry):
            _residual_layer_norm_rows(x_ref, o_ref, g_ref, b_ref, alpha, pl.multiple_of(c * group, group), group)
            return carry

        lax.fori_loop(0, x_ref.shape[0] // group, step, 0)


def _mlp_layer(x, w_up, w_down, ln_g, ln_b, *, alpha):
    rows, d = x.shape
    d_ff = w_up.shape[1]
    tm = min(MLP_ROW_BLOCK, rows)
    tf = min(MLP_FF_BLOCK, d_ff)
    assert rows % tm == 0 and d_ff % tf == 0 and tf % MLP_SUB_CHUNK == 0
    return pl.pallas_call(
        functools.partial(_mlp_kernel, alpha=alpha),
        out_shape=jax.ShapeDtypeStruct((rows, d), _F32),
        grid=(rows // tm, d_ff // tf),
        in_specs=[pl.BlockSpec((tm, d), lambda i, j: (i, 0)),
                  pl.BlockSpec((d, tf), lambda i, j: (0, j)),
                  pl.BlockSpec((tf, d), lambda i, j: (j, 0)),
                  _resident((1, d)), _resident((1, d))],
        out_specs=pl.BlockSpec((tm, d), lambda i, j: (i, 0)),
        scratch_shapes=[pltpu.VMEM((tm, d), _BF16)],
        compiler_params=_params("parallel", "arbitrary"),
        name="sqrelu_mlp",
    )(x, w_up, w_down, ln_g.reshape(1, d), ln_b.reshape(1, d))


def _deinterleave(res, o_ref, scratch, dil):
    if dil == 1:
        o_ref[0] = res.astype(o_ref.dtype)
        return
    res_ref = scratch[0]
    tm = res_ref.shape[1]
    n = tm // dil
    col_tiles = [slice(c * LANES, (c + 1) * LANES) for c in range(res_ref.shape[0])]
    for c, cols in enumerate(col_tiles):
        res_ref[c] = res[:, cols]
    if dil <= QKV_MAX_STRIDE:
        for r in range(dil):
            for c, cols in enumerate(col_tiles):
                o_ref[r, :, cols] = res_ref[c, pl.ds(r, n, stride=dil), :].astype(o_ref.dtype)
        return
    tmp_ref = scratch[1]
    s1 = QKV_MAX_STRIDE
    s2 = dil // s1
    for b in range(s1):
        for c in range(len(col_tiles)):
            tmp_ref[c, b * (tm // s1):(b + 1) * (tm // s1), :] = res_ref[c, pl.ds(b, tm // s1, stride=s1), :]
    for r in range(dil):
        for c, cols in enumerate(col_tiles):
            src = pl.ds((r % s1) * (tm // s1) + r // s1, n, stride=s2)
            o_ref[r, :, cols] = tmp_ref[c, src, :].astype(o_ref.dtype)


def _qkv_kernel(*refs, dil, emit_bf16):
    x_ref, w_refs, o_ref = refs[0], refs[1:4], refs[4]
    if emit_bf16:
        xb_ref, *scratch = refs[5:]
        xb_ref[...] = x_ref[...].astype(xb_ref.dtype)
    else:
        xb_ref, scratch = x_ref, refs[5:]
    for part, w_ref in enumerate(w_refs):
        res = jnp.dot(xb_ref[...], w_ref[...], preferred_element_type=_F32)
        _deinterleave(res, o_ref.at[part], scratch, dil)


def _qkv_group(x, w_qkv, *, batch, seq, group, n_groups, gw):
    rows, d = x.shape
    _, dil = ATTN_PATTERNS[group]
    emit_bf16 = x.dtype != _BF16
    tm = min(QKV_ROW_BLOCK, seq)
    assert seq % tm == 0 and tm % (dil * BF16_SUBLANES) == 0
    assert dil <= QKV_MAX_STRIDE or dil % QKV_MAX_STRIDE == 0
    blocks_per_seq = seq // tm
    x_block = pl.BlockSpec((tm, d), lambda i: (i, 0))
    w_blocks = [pl.BlockSpec((d, gw), lambda i, c=part * n_groups + group: (0, c), pipeline_mode=pl.Buffered(1))
                for part in range(3)]
    qkv_shape = jax.ShapeDtypeStruct((batch, 3, dil, seq // dil, gw), _BF16)
    qkv_block = pl.BlockSpec((None, 3, dil, tm // dil, gw),
                             lambda i: (i // blocks_per_seq, 0, 0, i % blocks_per_seq, 0))
    outs = pl.pallas_call(
        functools.partial(_qkv_kernel, dil=dil, emit_bf16=emit_bf16),
        out_shape=(qkv_shape, jax.ShapeDtypeStruct((rows, d), _BF16)) if emit_bf16 else qkv_shape,
        grid=(rows // tm,),
        in_specs=[x_block, *w_blocks],
        out_specs=(qkv_block, x_block) if emit_bf16 else qkv_block,
        scratch_shapes=[pltpu.VMEM((gw // LANES, tm, LANES), _F32)] * ((dil > 1) + (dil > QKV_MAX_STRIDE)),
        compiler_params=_params("parallel"),
        name=f"qkv_proj_g{group}",
    )(x, w_qkv, w_qkv, w_qkv)
    return outs if emit_bf16 else (outs, x)


def _attn_kernel(slope_ref, q_ref, k_ref, v_ref, o_ref, lse_ref, osc_ref, bias_ref, *, dil, n_back):
    sub_len = q_ref.shape[1]
    n_heads = q_ref.shape[2] // HEAD_DIM
    nb = sub_len // ATTN_BLOCK
    nq = ATTN_BLOCK
    nk = bias_ref.shape[-1]
    lane = lax.broadcasted_iota(jnp.int32, (nq, LANES), 1)
    qk_scale = LOG2_E / math.sqrt(HEAD_DIM)
    cols = [slice(h * HEAD_DIM, (h + 1) * HEAD_DIM) for h in range(n_heads)]
    ones = jnp.ones((nk, HEAD_DIM), _BF16)

    @pl.when(pl.program_id(0) == 0)
    def _():
        q_minus_k = lax.broadcasted_iota(jnp.int32, (nq, nk), 0) - lax.broadcasted_iota(jnp.int32, (nq, nk), 1)
        for i in range(bias_ref.shape[0]):
            dist = q_minus_k + i * nq
            valid = (dist >= 0) & (dist <= n_back)
            neg_dist = jnp.where(valid, -(dist * dil).astype(_F32), -MASK_DISTANCE)
            for h in range(n_heads):
                bias_ref[i, h] = (slope_ref[h] * LOG2_E) * neg_dist

    def block(p, carry):
        r = p // nb
        q0 = pl.multiple_of((p % nb) * nq, nq)
        k0 = pl.multiple_of(jnp.maximum(q0 - nq, 0), nq) if nb > 1 else 0
        back = (q0 - k0) // nq
        tok = pl.ds(q0 * dil + r, nq, stride=dil) if dil > 1 else pl.ds(q0, nq)
        ss = [lax.dot_general(q_ref[r, pl.ds(q0, nq), c], k_ref[r, pl.ds(k0, nk), c], (((1,), (1,)), ((), ())),
                              preferred_element_type=_F32) * qk_scale + bias_ref[back, h]
              for h, c in enumerate(cols)]
        ms = [jnp.max(s, axis=-1, keepdims=True) for s in ss]
        es = [jnp.exp2(s - m).astype(_BF16) for s, m in zip(ss, ms)]
        os = [jnp.dot(e, jnp.concatenate([v_ref[r, pl.ds(k0, nk), c], ones], axis=1), preferred_element_type=_F32)
              for e, c in zip(es, cols)]
        m_all = jnp.zeros((nq, LANES), _F32)
        den_all = jnp.ones((nq, LANES), _F32)
        for h in range(n_heads):
            den = os[h][:, HEAD_DIM:]
            osc_ref.at[h][tok, :] = os[h][:, :HEAD_DIM] * (1.0 / den)
            m_all = jnp.where(lane == h, ms[h], m_all)
            den_all = jnp.where(lane == h, den, den_all)
        lse_ref[tok, :] = (m_all + jnp.log2(den_all)) * LN_2
        return carry

    lax.fori_loop(0, dil * nb, block, 0)
    for h in range(n_heads):
        o_ref[:, cols[h]] = osc_ref[h].astype(o_ref.dtype)


def _attn_group(qkv, slopes, *, batch, seq, group, gw):
    window, dil = ATTN_PATTERNS[group]
    sub_len = seq // dil
    assert window // dil == ATTN_BLOCK and sub_len % ATTN_BLOCK == 0 and gw // HEAD_DIM <= LANES == HEAD_DIM
    n_heads = gw // HEAD_DIM
    windows = 2 if sub_len > ATTN_BLOCK else 1

    def part(j):
        return pl.BlockSpec((None, None, dil, sub_len, gw), lambda b: (b, j, 0, 0, 0))

    return pl.pallas_call(
        functools.partial(_attn_kernel, dil=dil, n_back=window // dil),
        out_shape=(jax.ShapeDtypeStruct((batch * seq, gw), _BF16),
                   jax.ShapeDtypeStruct((batch * seq, LANES), _F32)),
        grid=(batch,),
        in_specs=[pl.BlockSpec(memory_space=pltpu.SMEM), part(0), part(1), part(2)],
        out_specs=(pl.BlockSpec((seq, gw), lambda b: (b, 0)), pl.BlockSpec((seq, LANES), lambda b: (b, 0))),
        scratch_shapes=[pltpu.VMEM((n_heads, seq, HEAD_DIM), _F32),
                        pltpu.VMEM((windows, n_heads, ATTN_BLOCK, windows * ATTN_BLOCK), _F32)],
        compiler_params=_params("arbitrary"),
        name=f"dilated_attn_g{group}",
    )(slopes, qkv, qkv, qkv)


def _attn_out_kernel(*refs, alpha, n_groups):
    o_refs = refs[:n_groups]
    l_refs = refs[n_groups:2 * n_groups]
    x_ref, w_ref, g_ref, b_ref, out_ref = refs[2 * n_groups:]
    gw = w_ref.shape[0]
    expand = (lax.broadcasted_iota(jnp.int32, (2 * LANES, gw), 1) // HEAD_DIM
              == lax.broadcasted_iota(jnp.int32, (2 * LANES, gw), 0) % LANES).astype(_BF16)

    lses = [l[...] for l in l_refs]
    m = functools.reduce(jnp.maximum, lses)
    es = [jnp.exp(l - m) for l in lses]
    inv = 1.0 / functools.reduce(jnp.add, es)
    acc = None
    for e, o_ref in zip(es, o_refs):
        w = e * inv
        w_hi = w.astype(_BF16)
        w_lo = (w - w_hi.astype(_F32)).astype(_BF16)
        wide = jnp.dot(jnp.concatenate([w_hi, w_lo], axis=1), expand, preferred_element_type=_F32)
        term = wide * o_ref[...].astype(_F32)
        acc = term if acc is None else acc + term
    merged = acc.astype(_BF16)
    for first, size in _row_slabs(x_ref.shape[0], EPILOGUE_SLABS):
        out_ref[first:first + size, :] = jnp.dot(merged[first:first + size, :], w_ref[...],
                                                 preferred_element_type=_F32)
        _residual_layer_norm_rows(x_ref, out_ref, g_ref, b_ref, alpha, first, size)


def _alibi_slopes(n_groups, n_slots):
    n = n_groups * n_slots
    return jnp.exp2(-8.0 * jnp.arange(1, n + 1, dtype=_F32) / n).reshape(n_groups, n_slots)


def _attn_layer(x, w_qkv, w_out, ln_g, ln_b, *, batch, seq, alpha):
    rows, d = x.shape
    n_groups = len(ATTN_PATTERNS)
    gw = w_out.shape[0]
    assert w_qkv.shape[1] == 3 * n_groups * gw
    slopes = _alibi_slopes(n_groups, gw // HEAD_DIM)
    outs, lses = [], []
    x_mxu = x
    for g in range(n_groups):
        qkv, x_mxu = _qkv_group(x_mxu, w_qkv, batch=batch, seq=seq, group=g, n_groups=n_groups, gw=gw)
        o, lse = _attn_group(qkv, slopes[g], batch=batch, seq=seq, group=g, gw=gw)
        outs.append(o)
        lses.append(lse)
    tm = min(MERGE_ROW_BLOCK, rows)
    head_block = pl.BlockSpec((tm, gw), lambda i: (i, 0))
    lse_block = pl.BlockSpec((tm, LANES), lambda i: (i, 0))
    row_block = pl.BlockSpec((tm, d), lambda i: (i, 0))
    return pl.pallas_call(
        functools.partial(_attn_out_kernel, alpha=alpha, n_groups=n_groups),
        out_shape=jax.ShapeDtypeStruct((rows, d), _F32),
        grid=(rows // tm,),
        in_specs=[head_block] * n_groups + [lse_block] * n_groups
        + [row_block, _resident((gw, d)), _resident((1, d)), _resident((1, d))],
        out_specs=row_block,
        compiler_params=_params("parallel"),
        name="attn_merge_out",
    )(*outs, *lses, x, w_out.astype(_BF16), ln_g.reshape(1, d), ln_b.reshape(1, d))


def kernel(x, pool_w_in, pool_w_group, pool_scale, pool_w_out, attn_w_qkv, attn_w_out, mlp_w_up, mlp_w_down, ln_mix_g, ln_mix_b, ln_mlp_g, ln_mlp_b):
    batch, seq, d = x.shape
    depth = mlp_w_up.shape[0]
    alpha = (2 * depth) ** 0.25
    h = x.reshape(batch * seq, d)
    mlp_weights = {}
    w_qkv = None
    for i in range(depth):
        j = i // 2
        if i % 2 == 0:
            layers = [l for l in (i, i + 1) if l < depth]
            casts_a = [(mlp_w_up, l) for l in layers] + [(attn_w_qkv, j)] * (i + 1 < depth)
            h, ups, downs = _pool_layer(h, pool_w_in[j], pool_w_group[j], pool_scale[j], pool_w_out[j],
                                        ln_mix_g[i], ln_mix_b[i], casts_a,
                                        [(mlp_w_down, l) for l in layers], seq=seq, alpha=alpha)
            mlp_weights.update(zip(layers, zip(ups, downs)))
            w_qkv = ups[-1] if i + 1 < depth else None
        else:
            h = _attn_layer(h, w_qkv, attn_w_out[j], ln_mix_g[i], ln_mix_b[i], batch=batch, seq=seq, alpha=alpha)
        h = _mlp_layer(h, *mlp_weights[i], ln_mlp_g[i], ln_mlp_b[i], alpha=alpha)
    return h.reshape(batch, seq, d)
```

```python
import functools
import math

import jax
import jax.numpy as jnp
from jax import lax
from jax.experimental import pallas as pl
from jax.experimental.pallas import tpu as pltpu

POOL_WINDOWS = (2, 4, 8, 16)
MAX_POOL_WINDOW = max(POOL_WINDOWS)
ATTN_PATTERNS = ((128, 1), (512, 4), (2048, 16))
HEAD_DIM = 128
ATTN_BLOCK = 128
LN_EPS = 1e-5
MASK_DISTANCE = 1e30
LOG2_E = math.log2(math.e)
LN_2 = math.log(2.0)

LANES = 128
BF16_SUBLANES = 16
V7X_VMEM_LIMIT_BYTES = 60 * 1024 * 1024
LN_ROW_CHUNK = 32
LN_UNROLL = 4
EPILOGUE_SLABS = 4
POOL_ROW_BLOCK = 512
MLP_ROW_BLOCK = 1024
QKV_ROW_BLOCK = 1024
MERGE_ROW_BLOCK = 512
MLP_FF_BLOCK = 1024
QKV_MAX_STRIDE = 4

_F32 = jnp.float32
_BF16 = jnp.bfloat16


def _params(*semantics):
    return pltpu.CompilerParams(dimension_semantics=semantics, vmem_limit_bytes=V7X_VMEM_LIMIT_BYTES)


def _resident(shape):
    return pl.BlockSpec(shape, lambda *_: (0,) * len(shape), pipeline_mode=pl.Buffered(1))


def _residual_layer_norm_rows(x_ref, h_ref, g_ref, b_ref, alpha, first_row, n_rows):
    gam = g_ref[...]
    bet = b_ref[...]
    group = LN_ROW_CHUNK * LN_UNROLL
    assert n_rows % group == 0
    for offset in range(0, n_rows, group):
        chunks = [pl.ds(first_row + offset + u * LN_ROW_CHUNK, LN_ROW_CHUNK) for u in range(LN_UNROLL)]
        rs = [alpha * x_ref[rows, :] + h_ref[rows, :] for rows in chunks]
        xcs = [r - jnp.mean(r, axis=-1, keepdims=True) for r in rs]
        rstds = [lax.rsqrt(jnp.mean(xc * xc, axis=-1, keepdims=True) + LN_EPS) for xc in xcs]
        for rows, xc, rstd in zip(chunks, xcs, rstds):
            h_ref[rows, :] = xc * rstd * gam + bet


def _cast_specs(w_stack, layer, n_blocks):
    _, rows, cols = w_stack.shape
    assert rows % (n_blocks * BF16_SUBLANES) == 0
    block = rows // n_blocks
    return (pl.BlockSpec((None, block, cols), lambda i, *_: (layer, i, 0)),
            pl.BlockSpec((block, cols), lambda i, *_: (i, 0)),
            jax.ShapeDtypeStruct((rows, cols), _BF16))


def _row_slabs(n_rows, n_slabs):
    assert n_rows % n_slabs == 0
    size = n_rows // n_slabs
    return [(s * size, size) for s in range(n_slabs)]


def _cast_blocks(src_refs, dst_refs):
    for src_ref, dst_ref in zip(src_refs, dst_refs):
        dst_ref[...] = src_ref[...].astype(dst_ref.dtype)


def _pool_in_kernel(*refs, blocks_per_seq, n_casts):
    x_ref, w_ref = refs[:2]
    p_ref = refs[2 + n_casts]
    carry_ref = refs[-1]
    _cast_blocks(refs[2:2 + n_casts], refs[3 + n_casts:-1])
    tm = x_ref.shape[0]
    gd = w_ref.shape[1] // len(POOL_WINDOWS)
    blk = pl.program_id(0) % blocks_per_seq

    @pl.when(blk == 0)
    def _():
        carry_ref[...] = jnp.zeros_like(carry_ref)

    xb = x_ref[...].astype(_BF16)
    pos = lax.broadcasted_iota(jnp.int32, (tm, 1), 0) + blk * tm
    for g, w in reversed(list(enumerate(POOL_WINDOWS))):
        cols = slice(g * gd, (g + 1) * gd)
        u = jnp.dot(xb, w_ref[:, cols], preferred_element_type=_F32)
        t = jnp.concatenate([carry_ref[g], u], axis=0)
        carry_ref[g] = u[tm - MAX_POOL_WINDOW:, :]
        k = 1
        while k < w:
            t = t + pltpu.roll(t, k, 0)
            k *= 2
        inv_cnt = 1.0 / jnp.minimum(pos + 1, w).astype(_F32)
        p_ref[:, cols] = (t[MAX_POOL_WINDOW:, :] * inv_cnt - u).astype(p_ref.dtype)


def _pool_out_kernel(*refs, alpha, n_casts):
    p_ref, x_ref, wg_ref, scale_ref, wo_ref, g_ref, b_ref = refs[:7]
    o_ref = refs[7 + n_casts]
    _cast_blocks(refs[7:7 + n_casts], refs[8 + n_casts:])
    n_groups, gd, _ = wg_ref.shape
    ys = []
    for g in range(n_groups):
        cols = slice(g * gd, (g + 1) * gd)
        y = jnp.dot(p_ref[:, cols], wg_ref[g], preferred_element_type=_F32) * scale_ref[:, cols]
        ys.append(y.astype(_BF16))
    yb = jnp.concatenate(ys, axis=1)
    for first, size in _row_slabs(x_ref.shape[0], EPILOGUE_SLABS):
        o_ref[first:first + size, :] = jnp.dot(yb[first:first + size, :], wo_ref[...], preferred_element_type=_F32)
        _residual_layer_norm_rows(x_ref, o_ref, g_ref, b_ref, alpha, first, size)


def _pool_layer(x, w_in, w_group, scale, w_out, ln_g, ln_b, casts_a, casts_b, *, seq, alpha):
    rows, d = x.shape
    n_groups, gd, _ = w_group.shape
    tm = min(POOL_ROW_BLOCK, seq)
    assert seq % tm == 0 and tm >= MAX_POOL_WINDOW and gd * n_groups == d
    row_block = pl.BlockSpec((tm, d), lambda i: (i, 0))
    specs_a = [_cast_specs(*c, rows // tm) for c in casts_a]
    specs_b = [_cast_specs(*c, rows // tm) for c in casts_b]
    p, *a_bf16 = pl.pallas_call(
        functools.partial(_pool_in_kernel, blocks_per_seq=seq // tm, n_casts=len(casts_a)),
        out_shape=(jax.ShapeDtypeStruct((rows, d), _BF16), *[s[2] for s in specs_a]),
        grid=(rows // tm,),
        in_specs=[row_block, _resident((d, d)), *[s[0] for s in specs_a]],
        out_specs=(row_block, *[s[1] for s in specs_a]),
        scratch_shapes=[pltpu.VMEM((n_groups, MAX_POOL_WINDOW, gd), _F32)],
        compiler_params=_params("arbitrary"),
        name="pool_in",
    )(x, w_in.astype(_BF16), *[c[0] for c in casts_a])
    h, *b_bf16 = pl.pallas_call(
        functools.partial(_pool_out_kernel, alpha=alpha, n_casts=len(casts_b)),
        out_shape=(jax.ShapeDtypeStruct((rows, d), _F32), *[s[2] for s in specs_b]),
        grid=(rows // tm,),
        in_specs=[row_block, row_block, _resident((n_groups, gd, gd)), _resident((1, d)),
                  _resident((d, d)), _resident((1, d)), _resident((1, d)), *[s[0] for s in specs_b]],
        out_specs=(row_block, *[s[1] for s in specs_b]),
        compiler_params=_params("parallel"),
        name="pool_out",
    )(p, x, w_group.astype(_BF16), scale.reshape(1, d), w_out.astype(_BF16),
      ln_g.reshape(1, d), ln_b.reshape(1, d), *[c[0] for c in casts_b])
    return h, a_bf16, b_bf16


def _mlp_kernel(x_ref, wu_ref, wd_ref, g_ref, b_ref, o_ref, xb_ref, *, alpha):
    j = pl.program_id(1)

    @pl.when(j == 0)
    def _():
        xb_ref[...] = x_ref[...].astype(_BF16)
        o_ref[...] = jnp.zeros_like(o_ref)

    h = jnp.maximum(jnp.dot(xb_ref[...], wu_ref[...], preferred_element_type=_F32), 0.0)
    o_ref[...] += jnp.dot((h * h).astype(_BF16), wd_ref[...], preferred_element_type=_F32)

    @pl.when(j == pl.num_programs(1) - 1)
    def _():
        group = LN_ROW_CHUNK * LN_UNROLL

        def step(c, carry):
            _residual_layer_norm_rows(x_ref, o_ref, g_ref, b_ref, alpha, pl.multiple_of(c * group, group), group)
            return carry

        lax.fori_loop(0, x_ref.shape[0] // group, step, 0)


def _mlp_layer(x, w_up, w_down, ln_g, ln_b, *, alpha):
    rows, d = x.shape
    d_ff = w_up.shape[1]
    tm = min(MLP_ROW_BLOCK, rows)
    tf = min(MLP_FF_BLOCK, d_ff)
    assert rows % tm == 0 and d_ff % tf == 0
    return pl.pallas_call(
        functools.partial(_mlp_kernel, alpha=alpha),
        out_shape=jax.ShapeDtypeStruct((rows, d), _F32),
        grid=(rows // tm, d_ff // tf),
        in_specs=[pl.BlockSpec((tm, d), lambda i, j: (i, 0)),
                  pl.BlockSpec((d, tf), lambda i, j: (0, j)),
                  pl.BlockSpec((tf, d), lambda i, j: (j, 0)),
                  _resident((1, d)), _resident((1, d))],
        out_specs=pl.BlockSpec((tm, d), lambda i, j: (i, 0)),
        scratch_shapes=[pltpu.VMEM((tm, d), _BF16)],
        compiler_params=_params("parallel", "arbitrary"),
        name="sqrelu_mlp",
    )(x, w_up, w_down, ln_g.reshape(1, d), ln_b.reshape(1, d))


def _deinterleave(res, o_ref, scratch, dil):
    if dil == 1:
        o_ref[0] = res.astype(o_ref.dtype)
        return
    res_ref = scratch[0]
    tm = res_ref.shape[1]
    n = tm // dil
    col_tiles = [slice(c * LANES, (c + 1) * LANES) for c in range(res_ref.shape[0])]
    for c, cols in enumerate(col_tiles):
        res_ref[c] = res[:, cols]
    if dil <= QKV_MAX_STRIDE:
        for r in range(dil):
            for c, cols in enumerate(col_tiles):
                o_ref[r, :, cols] = res_ref[c, pl.ds(r, n, stride=dil), :].astype(o_ref.dtype)
        return
    tmp_ref = scratch[1]
    s1 = QKV_MAX_STRIDE
    s2 = dil // s1
    for b in range(s1):
        for c in range(len(col_tiles)):
            tmp_ref[c, b * (tm // s1):(b + 1) * (tm // s1), :] = res_ref[c, pl.ds(b, tm // s1, stride=s1), :]
    for r in range(dil):
        for c, cols in enumerate(col_tiles):
            src = pl.ds((r % s1) * (tm // s1) + r // s1, n, stride=s2)
            o_ref[r, :, cols] = tmp_ref[c, src, :].astype(o_ref.dtype)


def _qkv_kernel(*refs, dil, emit_bf16):
    x_ref, w_refs, o_ref = refs[0], refs[1:4], refs[4]
    if emit_bf16:
        xb_ref, *scratch = refs[5:]
        xb_ref[...] = x_ref[...].astype(xb_ref.dtype)
    else:
        xb_ref, scratch = x_ref, refs[5:]
    for part, w_ref in enumerate(w_refs):
        res = jnp.dot(xb_ref[...], w_ref[...], preferred_element_type=_F32)
        _deinterleave(res, o_ref.at[part], scratch, dil)


def _qkv_group(x, w_qkv, *, batch, seq, group, n_groups, gw):
    rows, d = x.shape
    _, dil = ATTN_PATTERNS[group]
    emit_bf16 = x.dtype != _BF16
    tm = min(QKV_ROW_BLOCK, seq)
    assert seq % tm == 0 and tm % (dil * BF16_SUBLANES) == 0
    assert dil <= QKV_MAX_STRIDE or dil % QKV_MAX_STRIDE == 0
    blocks_per_seq = seq // tm
    x_block = pl.BlockSpec((tm, d), lambda i: (i, 0))
    w_blocks = [pl.BlockSpec((d, gw), lambda i, c=part * n_groups + group: (0, c), pipeline_mode=pl.Buffered(1))
                for part in range(3)]
    qkv_shape = jax.ShapeDtypeStruct((batch, 3, dil, seq // dil, gw), _BF16)
    qkv_block = pl.BlockSpec((None, 3, dil, tm // dil, gw),
                             lambda i: (i // blocks_per_seq, 0, 0, i % blocks_per_seq, 0))
    outs = pl.pallas_call(
        functools.partial(_qkv_kernel, dil=dil, emit_bf16=emit_bf16),
        out_shape=(qkv_shape, jax.ShapeDtypeStruct((rows, d), _BF16)) if emit_bf16 else qkv_shape,
        grid=(rows // tm,),
        in_specs=[x_block, *w_blocks],
        out_specs=(qkv_block, x_block) if emit_bf16 else qkv_block,
        scratch_shapes=[pltpu.VMEM((gw // LANES, tm, LANES), _F32)] * ((dil > 1) + (dil > QKV_MAX_STRIDE)),
        compiler_params=_params("parallel"),
        name=f"qkv_proj_g{group}",
    )(x, w_qkv, w_qkv, w_qkv)
    return outs if emit_bf16 else (outs, x)


def _attn_kernel(slope_ref, q_ref, k_ref, v_ref, o_ref, lse_ref, osc_ref, bias_ref, *, dil, n_back):
    sub_len = q_ref.shape[1]
    n_heads = q_ref.shape[2] // HEAD_DIM
    nb = sub_len // ATTN_BLOCK
    nq = ATTN_BLOCK
    nk = bias_ref.shape[-1]
    lane = lax.broadcasted_iota(jnp.int32, (nq, LANES), 1)
    qk_scale = LOG2_E / math.sqrt(HEAD_DIM)
    cols = [slice(h * HEAD_DIM, (h + 1) * HEAD_DIM) for h in range(n_heads)]
    ones = jnp.ones((nk, HEAD_DIM), _BF16)

    @pl.when(pl.program_id(0) == 0)
    def _():
        q_minus_k = lax.broadcasted_iota(jnp.int32, (nq, nk), 0) - lax.broadcasted_iota(jnp.int32, (nq, nk), 1)
        for i in range(bias_ref.shape[0]):
            dist = q_minus_k + i * nq
            valid = (dist >= 0) & (dist <= n_back)
            neg_dist = jnp.where(valid, -(dist * dil).astype(_F32), -MASK_DISTANCE)
            for h in range(n_heads):
                bias_ref[i, h] = (slope_ref[h] * LOG2_E) * neg_dist

    def block(p, carry):
        r = p // nb
        q0 = pl.multiple_of((p % nb) * nq, nq)
        k0 = pl.multiple_of(jnp.maximum(q0 - nq, 0), nq) if nb > 1 else 0
        back = (q0 - k0) // nq
        tok = pl.ds(q0 * dil + r, nq, stride=dil) if dil > 1 else pl.ds(q0, nq)
        ss = [lax.dot_general(q_ref[r, pl.ds(q0, nq), c], k_ref[r, pl.ds(k0, nk), c], (((1,), (1,)), ((), ())),
                              preferred_element_type=_F32) * qk_scale + bias_ref[back, h]
              for h, c in enumerate(cols)]
        ms = [jnp.max(s, axis=-1, keepdims=True) for s in ss]
        es = [jnp.exp2(s - m).astype(_BF16) for s, m in zip(ss, ms)]
        os = [jnp.dot(e, jnp.concatenate([v_ref[r, pl.ds(k0, nk), c], ones], axis=1), preferred_element_type=_F32)
              for e, c in zip(es, cols)]
        m_all = jnp.zeros((nq, LANES), _F32)
        den_all = jnp.ones((nq, LANES), _F32)
        for h in range(n_heads):
            den = os[h][:, HEAD_DIM:]
            osc_ref.at[h][tok, :] = os[h][:, :HEAD_DIM] * (1.0 / den)
            m_all = jnp.where(lane == h, ms[h], m_all)
            den_all = jnp.where(lane == h, den, den_all)
        lse_ref[tok, :] = (m_all + jnp.log2(den_all)) * LN_2
        return carry

    lax.fori_loop(0, dil * nb, block, 0)
    for h in range(n_heads):
        o_ref[:, cols[h]] = osc_ref[h].astype(o_ref.dtype)


def _attn_group(qkv, slopes, *, batch, seq, group, gw):
    window, dil = ATTN_PATTERNS[group]
    sub_len = seq // dil
    assert window // dil == ATTN_BLOCK and sub_len % ATTN_BLOCK == 0 and gw // HEAD_DIM <= LANES == HEAD_DIM
    n_heads = gw // HEAD_DIM
    windows = 2 if sub_len > ATTN_BLOCK else 1

    def part(j):
        return pl.BlockSpec((None, None, dil, sub_len, gw), lambda b: (b, j, 0, 0, 0))

    return pl.pallas_call(
        functools.partial(_attn_kernel, dil=dil, n_back=window // dil),
        out_shape=(jax.ShapeDtypeStruct((batch * seq, gw), _BF16),
                   jax.ShapeDtypeStruct((batch * seq, LANES), _F32)),
        grid=(batch,),
        in_specs=[pl.BlockSpec(memory_space=pltpu.SMEM), part(0), part(1), part(2)],
        out_specs=(pl.BlockSpec((seq, gw), lambda b: (b, 0)), pl.BlockSpec((seq, LANES), lambda b: (b, 0))),
        scratch_shapes=[pltpu.VMEM((n_heads, seq, HEAD_DIM), _F32),
                        pltpu.VMEM((windows, n_heads, ATTN_BLOCK, windows * ATTN_BLOCK), _F32)],
        compiler_params=_params("arbitrary"),
        name=f"dilated_attn_g{group}",
    )(slopes, qkv, qkv, qkv)


def _attn_out_kernel(*refs, alpha, n_groups):
    o_refs = refs[:n_groups]
    l_refs = refs[n_groups:2 * n_groups]
    x_ref, w_ref, g_ref, b_ref, out_ref = refs[2 * n_groups:]
    gw = w_ref.shape[0]
    expand = (lax.broadcasted_iota(jnp.int32, (2 * LANES, gw), 1) // HEAD_DIM
              == lax.broadcasted_iota(jnp.int32, (2 * LANES, gw), 0) % LANES).astype(_BF16)

    lses = [l[...] for l in l_refs]
    m = functools.reduce(jnp.maximum, lses)
    es = [jnp.exp(l - m) for l in lses]
    inv = 1.0 / functools.reduce(jnp.add, es)
    acc = None
    for e, o_ref in zip(es, o_refs):
        w = e * inv
        w_hi = w.astype(_BF16)
        w_lo = (w - w_hi.astype(_F32)).astype(_BF16)
        wide = jnp.dot(jnp.concatenate([w_hi, w_lo], axis=1), expand, preferred_element_type=_F32)
        term = wide * o_ref[...].astype(_F32)
        acc = term if acc is None else acc + term
    merged = acc.astype(_BF16)
    for first, size in _row_slabs(x_ref.shape[0], EPILOGUE_SLABS):
        out_ref[first:first + size, :] = jnp.dot(merged[first:first + size, :], w_ref[...],
                                                 preferred_element_type=_F32)
        _residual_layer_norm_rows(x_ref, out_ref, g_ref, b_ref, alpha, first, size)


def _alibi_slopes(n_groups, n_slots):
    n = n_groups * n_slots
    return jnp.exp2(-8.0 * jnp.arange(1, n + 1, dtype=_F32) / n).reshape(n_groups, n_slots)


def _attn_layer(x, w_qkv, w_out, ln_g, ln_b, *, batch, seq, alpha):
    rows, d = x.shape
    n_groups = len(ATTN_PATTERNS)
    gw = w_out.shape[0]
    assert w_qkv.shape[1] == 3 * n_groups * gw
    slopes = _alibi_slopes(n_groups, gw // HEAD_DIM)
    outs, lses = [], []
    x_mxu = x
    for g in range(n_groups):
        qkv, x_mxu = _qkv_group(x_mxu, w_qkv, batch=batch, seq=seq, group=g, n_groups=n_groups, gw=gw)
        o, lse = _attn_group(qkv, slopes[g], batch=batch, seq=seq, group=g, gw=gw)
        outs.append(o)
        lses.append(lse)
    tm = min(MERGE_ROW_BLOCK, rows)
    head_block = pl.BlockSpec((tm, gw), lambda i: (i, 0))
    lse_block = pl.BlockSpec((tm, LANES), lambda i: (i, 0))
    row_block = pl.BlockSpec((tm, d), lambda i: (i, 0))
    return pl.pallas_call(
        functools.partial(_attn_out_kernel, alpha=alpha, n_groups=n_groups),
        out_shape=jax.ShapeDtypeStruct((rows, d), _F32),
        grid=(rows // tm,),
        in_specs=[head_block] * n_groups + [lse_block] * n_groups
        + [row_block, _resident((gw, d)), _resident((1, d)), _resident((1, d))],
        out_specs=row_block,
        compiler_params=_params("parallel"),
        name="attn_merge_out",
    )(*outs, *lses, x, w_out.astype(_BF16), ln_g.reshape(1, d), ln_b.reshape(1, d))


def kernel(x, pool_w_in, pool_w_group, pool_scale, pool_w_out, attn_w_qkv, attn_w_out, mlp_w_up, mlp_w_down, ln_mix_g, ln_mix_b, ln_mlp_g, ln_mlp_b):
    batch, seq, d = x.shape
    depth = mlp_w_up.shape[0]
    alpha = (2 * depth) ** 0.25
    h = x.reshape(batch * seq, d)
    mlp_weights = {}
    w_qkv = None
    for i in range(depth):
        j = i // 2
        if i % 2 == 0:
            layers = [l for l in (i, i + 1) if l < depth]
            casts_a = [(mlp_w_up, l) for l in layers] + [(attn_w_qkv, j)] * (i + 1 < depth)
            h, ups, downs = _pool_layer(h, pool_w_in[j], pool_w_group[j], pool_scale[j], pool_w_out[j],
                                        ln_mix_g[i], ln_mix_b[i], casts_a,
                                        [(mlp_w_down, l) for l in layers], seq=seq, alpha=alpha)
            mlp_weights.update(zip(layers, zip(ups, downs)))
            w_qkv = ups[-1] if i + 1 < depth else None
        else:
            h = _attn_layer(h, w_qkv, attn_w_out[j], ln_mix_g[i], ln_mix_b[i], batch=batch, seq=seq, alpha=alpha)
        h = _mlp_layer(h, *mlp_weights[i], ln_mlp_g[i], ln_mlp_b[i], alpha=alpha)
    return h.reshape(batch, seq, d)
```
